```python
import jax, jax.numpy as jnp
from jax import lax
import numpy as np

D_MODEL = 2048
BATCH = 1
SEQ = 16384
DEPTH = 1

ATTN_HEADS = 8
ATTN_HEAD_DIM = 128
ATTN_WIDTH = ATTN_HEADS * ATTN_HEAD_DIM
DILATED_PATTERNS = ((128, 1), (512, 4), (2048, 16))
Q_BLOCK = 128
ROPE_THETA = 500000.0
ROT_DIM = ATTN_HEAD_DIM // 4
HGRN_HEADS = 8
HGRN_EXPAND = 128
HGRN_HEAD_V = 128
HGRN_KEY_WIDTH = HGRN_HEADS * HGRN_EXPAND
HGRN_VAL_WIDTH = HGRN_HEADS * HGRN_HEAD_V
HGRN_CHUNK = 64
MIX_WIDTH = ATTN_WIDTH + HGRN_VAL_WIDTH
IN_WIDTH = 3 * ATTN_WIDTH + 2 * HGRN_KEY_WIDTH + 2 * HGRN_VAL_WIDTH
D_FF = 5632
CONV_WIDTH = 3
N_MOD = 6
EPS = 1e-6

kernel_name = "hybrid_dilated_swa_hgrn2_convffn_adaln"


def rms_norm(x, w):
    xf = x.astype(jnp.float32)
    y = xf * lax.rsqrt(jnp.mean(xf * xf, axis=-1, keepdims=True) + EPS)
    return y.astype(x.dtype) * w


def partial_rotary(x, positions):
    inv_freq = 1.0 / (ROPE_THETA ** (jnp.arange(0, ROT_DIM, 2, dtype=jnp.float32) / ROT_DIM))
    ang = positions.astype(jnp.float32)[..., None] * inv_freq
    cos = jnp.cos(ang)[:, :, None, :]
    sin = jnp.sin(ang)[:, :, None, :]
    xf = x[..., :ROT_DIM].astype(jnp.float32)
    x1, x2 = xf[..., : ROT_DIM // 2], xf[..., ROT_DIM // 2:]
    rot = jnp.concatenate([x1 * cos - x2 * sin, x2 * cos + x1 * sin], axis=-1).astype(x.dtype)
    return jnp.concatenate([rot, x[..., ROT_DIM:]], axis=-1)


def dilated_window_attention(q, k, v):
    B, S, H, Dh = q.shape
    nblk = S // Q_BLOCK
    scale = Dh ** -0.5
    qb = q.reshape(B, nblk, Q_BLOCK, H, Dh).transpose(1, 0, 2, 3, 4)

    def one_block(args):
        blk, q_blk = args
        t = blk * Q_BLOCK + jnp.arange(Q_BLOCK, dtype=jnp.int32)
        outs, lses = [], []
        for (w, d) in DILATED_PATTERNS:
            j = jnp.arange(w // d + 1, dtype=jnp.int32)
            idx = t[:, None] - d * j[None, :]
            valid = idx >= 0
            idxc = jnp.maximum(idx, 0)
            kg = jnp.take(k, idxc, axis=1)
            vg = jnp.take(v, idxc, axis=1)
            s = jnp.einsum('bqhd,bqjhd->bhqj', q_blk, kg).astype(jnp.float32) * scale
            s = jnp.where(valid[None, None], s, -jnp.inf)
            lse = jax.nn.logsumexp(s, axis=-1)
            p = jnp.exp(s - lse[..., None])
            outs.append(jnp.einsum('bhqj,bqjhd->bqhd', p.astype(v.dtype), vg).astype(jnp.float32))
            lses.append(lse)
        wmix = jax.nn.softmax(jnp.stack(lses, axis=0), axis=0)
        wmix = wmix.transpose(0, 1, 3, 2)[..., None]
        return jnp.sum(wmix * jnp.stack(outs, axis=0), axis=0).astype(q.dtype)

    o = lax.map(one_block, (jnp.arange(nblk, dtype=jnp.int32), qb))
    return o.transpose(1, 0, 2, 3, 4).reshape(B, S, H * Dh)


def hgrn2(q, f_logit, i, g, lb, gnorm_w):
    B, S, _ = q.shape
    H, K, V, C = HGRN_HEADS, HGRN_EXPAND, HGRN_HEAD_V, HGRN_CHUNK
    n = S // C
    f = lb + (1.0 - lb) * jax.nn.sigmoid(f_logit.astype(jnp.float32))
    log_f = jnp.log(f)
    kk = 1.0 - f
    qf = jax.nn.silu(q.astype(jnp.float32)) * (K ** -0.5)
    vf = i.astype(jnp.float32)

    def to_chunks(a, dim):
        return a.reshape(B, n, C, H, dim).transpose(1, 0, 3, 2, 4)

    qc, kc, lc, vc = to_chunks(qf, K), to_chunks(kk, K), to_chunks(log_f, K), to_chunks(vf, V)
    causal = jnp.tril(jnp.ones((C, C), dtype=bool))[:, :, None]

    def step(state, inp):
        q_c, k_c, l_c, v_c = inp
        b = jnp.cumsum(l_c, axis=2)
        decay = jnp.exp(jnp.where(causal, b[:, :, :, None, :] - b[:, :, None, :, :], -jnp.inf))
        a = jnp.einsum('bhtk,bhsk,bhtsk->bhts', q_c, k_c, decay)
        o = (jnp.einsum('bhtk,bhkv->bhtv', q_c * jnp.exp(b), state)
             + jnp.einsum('bhts,bhsv->bhtv', a, v_c))
        b_last = b[:, :, -1:, :]
        new_state = (jnp.exp(b_last[:, :, 0, :])[..., None] * state
                     + jnp.einsum('bhsk,bhsv->bhkv', k_c * jnp.exp(b_last - b), v_c))
        return new_state, o

    s0 = jnp.zeros((B, H, K, V), jnp.float32)
    _, o = lax.scan(step, s0, (qc, kc, lc, vc))
    o = o.transpose(1, 0, 3, 2, 4).reshape(B, S, H, V)
    o = o * lax.rsqrt(jnp.mean(o * o, axis=-1, keepdims=True) + EPS) * gnorm_w.astype(jnp.float32)
    o = o * jax.nn.silu(g.astype(jnp.float32)).reshape(B, S, H, V)
    return o.reshape(B, S, H * V).astype(q.dtype)


def causal_depthwise_conv(u, w, b):
    S = u.shape[1]
    kw = w.shape[0]
    up = jnp.pad(u, ((0, 0), (kw - 1, 0), (0, 0)))
    return sum(up[:, j:j + S] * w[j] for j in range(kw)) + b


def setup_inputs(seed: int = 0) -> dict:
    key = jax.random.key(seed)
    ks = jax.random.split(key, 16)
    f32 = jnp.float32
    nrm = lambda k, shape, s: jax.random.normal(k, shape, f32) * s
    x = jax.random.normal(ks[0], (BATCH, SEQ, D_MODEL), f32)
    c = jax.random.normal(ks[1], (BATCH, D_MODEL), f32)
    offset = jax.random.randint(ks[2], (BATCH, 1), 0, 1024, dtype=jnp.int32)
    positions = offset + jnp.arange(SEQ, dtype=jnp.int32)[None, :]
    return {
        "x": x,
        "c": c,
        "positions": positions,
        "w_ada": nrm(ks[3], (DEPTH, D_MODEL, N_MOD * D_MODEL), 0.5 * D_MODEL ** -0.5),
        "b_ada": nrm(ks[4], (DEPTH, N_MOD * D_MODEL), 0.01),
        "norm1_w": 1.0 + nrm(ks[5], (DEPTH, D_MODEL), 0.02),
        "w_in": nrm(ks[6], (DEPTH, D_MODEL, IN_WIDTH), D_MODEL ** -0.5),
        "lb_logits": nrm(ks[7], (DEPTH + 1, HGRN_KEY_WIDTH), 0.5),
        "gnorm_w": 1.0 + nrm(ks[8], (DEPTH, HGRN_HEAD_V), 0.02),
        "w_out": nrm(ks[9], (DEPTH, MIX_WIDTH, D_MODEL), MIX_WIDTH ** -0.5),
        "norm2_w": 1.0 + nrm(ks[10], (DEPTH, D_MODEL), 0.02),
        "w_up": nrm(ks[11], (DEPTH, D_MODEL, 2 * D_FF), D_MODEL ** -0.5),
        "conv_w": nrm(ks[12], (DEPTH, CONV_WIDTH, 2 * D_FF), CONV_WIDTH ** -0.5),
        "conv_b": nrm(ks[13], (DEPTH, 2 * D_FF), 0.01),
        "w_down": nrm(ks[14], (DEPTH, D_FF, D_MODEL), D_FF ** -0.5),
        "final_norm_w": 1.0 + nrm(ks[15], (D_MODEL,), 0.02),
    }


def reference(x, c, positions, w_ada, b_ada, norm1_w, w_in, lb_logits, gnorm_w, w_out,
              norm2_w, w_up, conv_w, conv_b, w_down, final_norm_w):
    B, S, _ = x.shape
    lb_all = jnp.cumsum(jax.nn.softmax(lb_logits.astype(jnp.float32), axis=0), axis=0)
    c_act = jax.nn.silu(c)
    for l in range(DEPTH):
        mod = c_act @ w_ada[l] + b_ada[l]
        sh1, sc1, g1, sh2, sc2, g2 = [m[:, None, :] for m in jnp.split(mod, N_MOD, axis=-1)]

        h = rms_norm(x, norm1_w[l]) * (1.0 + sc1) + sh1
        proj = h @ w_in[l]
        o0 = 0
        aq = proj[..., o0:o0 + ATTN_WIDTH]; o0 += ATTN_WIDTH
        ak = proj[..., o0:o0 + ATTN_WIDTH]; o0 += ATTN_WIDTH
        av = proj[..., o0:o0 + ATTN_WIDTH]; o0 += ATTN_WIDTH
        hq = proj[..., o0:o0 + HGRN_KEY_WIDTH]; o0 += HGRN_KEY_WIDTH
        hf = proj[..., o0:o0 + HGRN_KEY_WIDTH]; o0 += HGRN_KEY_WIDTH
        hi = proj[..., o0:o0 + HGRN_VAL_WIDTH]; o0 += HGRN_VAL_WIDTH
        hg = proj[..., o0:o0 + HGRN_VAL_WIDTH]

        qa = partial_rotary(aq.reshape(B, S, ATTN_HEADS, ATTN_HEAD_DIM), positions)
        ka = partial_rotary(ak.reshape(B, S, ATTN_HEADS, ATTN_HEAD_DIM), positions)
        va = av.reshape(B, S, ATTN_HEADS, ATTN_HEAD_DIM)
        y_attn = dilated_window_attention(qa, ka, va)
        y_rec = hgrn2(hq, hf, hi, hg, lb_all[l], gnorm_w[l])
        y_mix = jnp.concatenate([y_attn, y_rec], axis=-1) @ w_out[l]
        x = x + g1 * y_mix

        h = rms_norm(x, norm2_w[l]) * (1.0 + sc2) + sh2
        u = causal_depthwise_conv(h @ w_up[l], conv_w[l], conv_b[l])
        gate, up = u[..., :D_FF], u[..., D_FF:]
        x = x + g2 * ((jax.nn.silu(gate) * up) @ w_down[l])
    return rms_norm(x, final_norm_w)
```

```python
import functools

import jax
import jax.numpy as jnp
from jax import lax
from jax.experimental import pallas as pl
from jax.experimental.pallas import tpu as pltpu

F32 = jnp.float32
BF16 = jnp.bfloat16

D_MODEL = 2048
N_HEADS = 8
HEAD_DIM = 128
WIDTH = N_HEADS * HEAD_DIM
N_SEG = 7
ROT_DIM = HEAD_DIM // 4
ROPE_THETA = 500000.0
DILATIONS = (1, 4, 16)
WINDOW_STEPS = 128
D_FF = 5632
CONV_WIDTH = 3
N_MOD = 6
EPS = 1e-6
MASKED = -1e30

VMEM_LIMIT = 56 * 1024 * 1024


def _sigmoid(x):
    return 1.0 / (1.0 + jnp.exp(-x))


def _silu(x):
    return x * _sigmoid(x)


def _params(*semantics):
    return pltpu.CompilerParams(dimension_semantics=semantics,
                                vmem_limit_bytes=VMEM_LIMIT)


def _mod_kernel(c_ref, w_ref, b_ref, o_ref):
    ca = _silu(c_ref[...])
    o_ref[...] = jnp.sum(w_ref[...] * ca, axis=0, keepdims=True) + b_ref[...]


def _modulation(c_col, w_ada, b_ada):
    d, n = w_ada.shape
    tn = 512
    return pl.pallas_call(
        _mod_kernel,
        out_shape=jax.ShapeDtypeStruct((1, n), F32),
        grid=(n // tn,),
        in_specs=[pl.BlockSpec((d, 1), lambda j: (0, 0)),
                  pl.BlockSpec((d, tn), lambda j: (0, j)),
                  pl.BlockSpec((1, tn), lambda j: (0, j))],
        out_specs=pl.BlockSpec((1, tn), lambda j: (0, j)),
        compiler_params=_params("parallel"),
        name="mod",
    )(c_col, w_ada, b_ada)


def _norm_modulate(x, norm_w, scale, shift):
    ms = jnp.mean(x * x, axis=-1, keepdims=True)
    return (x * lax.rsqrt(ms + EPS)) * norm_w * (1.0 + scale) + shift


def _in_kernel(x_ref, pos_ref, invf_ref, nw_ref, sh_ref, sc_ref, lbl_ref, w_ref,
               o_ref, h_ref, cos_ref, sina_ref, sinb_ref):
    j = pl.program_id(1)

    @pl.when(j == 0)
    def _():
        h = _norm_modulate(x_ref[...], nw_ref[...], sc_ref[...], sh_ref[...])
        h_ref[...] = h.astype(BF16)
        ang = pos_ref[...].astype(F32) * invf_ref[...]
        lane = lax.broadcasted_iota(jnp.int32, ang.shape, 1)
        c = jnp.cos(ang)
        s = jnp.sin(ang)
        half = ROT_DIM // 2
        cos_ref[...] = jnp.where(lane < ROT_DIM, c, 1.0)
        sina_ref[...] = jnp.where(lane < half, -s, 0.0)
        sinb_ref[...] = jnp.where((lane >= half) & (lane < ROT_DIM), s, 0.0)

    r = jnp.dot(h_ref[...], w_ref[...], preferred_element_type=F32)

    def rotary(scale):
        half = ROT_DIM // 2
        for hh in range(N_HEADS):
            sl = slice(hh * HEAD_DIM, (hh + 1) * HEAD_DIM)
            rh = r[:, sl]
            out = (rh * cos_ref[...]
                   + pltpu.roll(rh, HEAD_DIM - half, 1) * sina_ref[...]
                   + pltpu.roll(rh, half, 1) * sinb_ref[...])
            o_ref[:, sl] = out * scale

    @pl.when(j == 0)
    def _():
        rotary(HEAD_DIM ** -0.5)

    @pl.when(j == 1)
    def _():
        rotary(1.0)

    @pl.when((j == 2) | (j == 5))
    def _():
        o_ref[...] = r

    @pl.when(j == 3)
    def _():
        o_ref[...] = _silu(r) * (HEAD_DIM ** -0.5)

    @pl.when(j == 4)
    def _():
        lg = lbl_ref[...]
        e = jnp.exp(lg - jnp.max(lg, axis=0, keepdims=True))
        lb = e[0:1, :] / jnp.sum(e, axis=0, keepdims=True)
        f = lb + (1.0 - lb) * _sigmoid(r)
        o_ref[...] = jnp.log(f)

    @pl.when(j == 6)
    def _():
        o_ref[...] = _silu(r)


def _in_proj(x2d, pos_col, invf, norm_w, shift, scale, lb_logits, w_in_bf16, tm=512):
    s, d = x2d.shape
    n = w_in_bf16.shape[1]
    return pl.pallas_call(
        _in_kernel,
        out_shape=jax.ShapeDtypeStruct((s, n), F32),
        grid=(s // tm, n // WIDTH),
        in_specs=[pl.BlockSpec((tm, d), lambda i, j: (i, 0)),
                  pl.BlockSpec((tm, 1), lambda i, j: (i, 0)),
                  pl.BlockSpec((1, HEAD_DIM), lambda i, j: (0, 0)),
                  pl.BlockSpec((1, d), lambda i, j: (0, 0)),
                  pl.BlockSpec((1, d), lambda i, j: (0, 0)),
                  pl.BlockSpec((1, d), lambda i, j: (0, 1)),
                  pl.BlockSpec(lb_logits.shape, lambda i, j: (0, 0)),
                  pl.BlockSpec((d, WIDTH), lambda i, j: (0, j))],
        out_specs=pl.BlockSpec((tm, WIDTH), lambda i, j: (i, j)),
        scratch_shapes=[pltpu.VMEM((tm, d), BF16),
                        pltpu.VMEM((tm, HEAD_DIM), F32),
                        pltpu.VMEM((tm, HEAD_DIM), F32),
                        pltpu.VMEM((tm, HEAD_DIM), F32)],
        compiler_params=_params("parallel", "arbitrary"),
        name="in_proj",
    )(x2d, pos_col, invf, norm_w, shift, scale, lb_logits, w_in_bf16)


ATT_TILE = 2048
ATT_BLK = 128
ATT_ITERS = ATT_TILE // ATT_BLK


def _attn_kernel(q_ref, kp_ref, kc_ref, vp_ref, vc_ref, o_ref,
                 kbuf, vbuf, acc_ref, m_ref, l_ref):
    t = pl.program_id(1)
    kbuf[0:ATT_TILE, :] = kp_ref[...]
    kbuf[ATT_TILE:, :] = kc_ref[...]
    vbuf[0:ATT_TILE, :] = vp_ref[...]
    vbuf[ATT_TILE:, :] = vc_ref[...]

    qi = lax.broadcasted_iota(jnp.int32, (ATT_BLK, 2 * ATT_BLK), 0)
    ki = lax.broadcasted_iota(jnp.int32, (ATT_BLK, 2 * ATT_BLK), 1)
    band = (ki >= qi) & (ki <= qi + WINDOW_STEPS)
    cur_half = ki >= ATT_BLK

    for p, d in enumerate(DILATIONS):
        def body(idx, carry, p=p, d=d):
            r = idx % d
            blk = idx // d
            start = r + blk * (ATT_BLK * d)
            qb = q_ref[pl.ds(start, ATT_BLK, stride=d), :].astype(BF16)
            kstart = ATT_TILE + start - ATT_BLK * d
            kk = kbuf[pl.ds(kstart, 2 * ATT_BLK, stride=d), :].astype(BF16)
            vv = vbuf[pl.ds(kstart, 2 * ATT_BLK, stride=d), :].astype(BF16)
            s = lax.dot_general(qb, kk, (((1,), (1,)), ((), ())),
                                preferred_element_type=F32)
            has_prev = jnp.logical_or(t > 0, blk > 0)
            valid = band & (cur_half | has_prev)
            s = jnp.where(valid, s, MASKED)
            m = jnp.max(s, axis=1, keepdims=True)
            pe = jnp.exp(s - m)
            l = jnp.sum(pe, axis=1, keepdims=True)
            acc = jnp.dot(pe.astype(BF16), vv, preferred_element_type=F32)
            rows = pl.ds(start, ATT_BLK, stride=d)
            acc_ref[p, rows, :] = acc
            m_ref[p, rows, :] = jnp.broadcast_to(m, (ATT_BLK, HEAD_DIM))
            l_ref[p, rows, :] = jnp.broadcast_to(l, (ATT_BLK, HEAD_DIM))
            return carry
        lax.fori_loop(0, ATT_ITERS, body, 0)

    m_all = jnp.maximum(jnp.maximum(m_ref[0], m_ref[1]), m_ref[2])
    num = jnp.zeros((ATT_TILE, HEAD_DIM), F32)
    den = jnp.zeros((ATT_TILE, HEAD_DIM), F32)
    for p in range(len(DILATIONS)):
        w = jnp.exp(m_ref[p] - m_all)
        num = num + w * acc_ref[p]
        den = den + w * l_ref[p]
    o_ref[...] = (num / den).astype(o_ref.dtype)


def _attention(proj):
    s = proj.shape[0]
    nt = s // ATT_TILE
    blk = (ATT_TILE, HEAD_DIM)
    prev = lambda t: jnp.maximum(t - 1, 0)
    return pl.pallas_call(
        _attn_kernel,
        out_shape=jax.ShapeDtypeStruct((s, WIDTH), BF16),
        grid=(N_HEADS, nt),
        in_specs=[pl.BlockSpec(blk, lambda h, t: (t, h)),
                  pl.BlockSpec(blk, lambda h, t: (prev(t), N_HEADS + h)),
                  pl.BlockSpec(blk, lambda h, t: (t, N_HEADS + h)),
                  pl.BlockSpec(blk, lambda h, t: (prev(t), 2 * N_HEADS + h)),
                  pl.BlockSpec(blk, lambda h, t: (t, 2 * N_HEADS + h))],
        out_specs=pl.BlockSpec(blk, lambda h, t: (t, h)),
        scratch_shapes=[pltpu.VMEM((2 * ATT_TILE, HEAD_DIM), F32),
                        pltpu.VMEM((2 * ATT_TILE, HEAD_DIM), F32),
                        pltpu.VMEM((len(DILATIONS), ATT_TILE, HEAD_DIM), F32),
                        pltpu.VMEM((len(DILATIONS), ATT_TILE, HEAD_DIM), F32),
                        pltpu.VMEM((len(DILATIONS), ATT_TILE, HEAD_DIM), F32)],
        compiler_params=_params("parallel", "parallel"),
        name="attn",
    )(proj, proj, proj, proj, proj)


REC_CHUNK = 128
REC_LEVELS = 7
REC_ROWS = 1024
SUBLANES = 8


def _block_scans(lf):
    c = lf.shape[0]
    row = lax.broadcasted_iota(jnp.int32, (c, HEAD_DIM), 0)
    fwd, rev = [lf], [lf]
    x, y = lf, lf
    for j in range(REC_LEVELS):
        m = 1 << j
        if m < SUBLANES:
            x3 = x.reshape(c // SUBLANES, SUBLANES, HEAD_DIM)
            y3 = y.reshape(c // SUBLANES, SUBLANES, HEAD_DIM)
            addx = jnp.zeros_like(x)
            addy = jnp.zeros_like(y)
            for k in range(m):
                sx = pltpu.roll(x3, k + 1, 1).reshape(c, HEAD_DIM)
                addx = jnp.where((row & (2 * m - 1)) == m + k, sx, addx)
                sy = pltpu.roll(y3, SUBLANES - (k + 1), 1).reshape(c, HEAD_DIM)
                addy = jnp.where((row & (2 * m - 1)) == m - 1 - k, sy, addy)
            x = x + addx
            y = y + addy
        else:
            xs, ys = [], []
            for b in range(c // (2 * m)):
                lo = x[b * 2 * m: b * 2 * m + m]
                hi = x[b * 2 * m + m: (b + 1) * 2 * m]
                xs += [lo, hi + lo[m - 1:m, :]]
                lo = y[b * 2 * m: b * 2 * m + m]
                hi = y[b * 2 * m + m: (b + 1) * 2 * m]
                ys += [lo + hi[0:1, :], hi]
            x = jnp.concatenate(xs, axis=0)
            y = jnp.concatenate(ys, axis=0)
        fwd.append(x)
        rev.append(y)
    return fwd, rev


def _rec_kernel(q_ref, lf_ref, v_ref, g_ref, gw_ref, o_ref, state_ref):
    @pl.when(pl.program_id(1) == 0)
    def _():
        state_ref[...] = jnp.zeros_like(state_ref)

    c = REC_CHUNK
    ti = lax.broadcasted_iota(jnp.int32, (c, c), 0)
    si = lax.broadcasted_iota(jnp.int32, (c, c), 1)
    nt_dims = (((1,), (1,)), ((), ()))

    def chunk(ci, carry):
        rows = pl.ds(pl.multiple_of(ci * c, c), c)
        qf = q_ref[rows, :]
        lf = lf_ref[rows, :]
        v = v_ref[rows, :].astype(BF16)
        kk = 1.0 - jnp.exp(lf)
        fwd, rev = _block_scans(lf)
        b = fwd[REC_LEVELS]

        a = jnp.where(ti == si,
                      lax.dot_general(qf.astype(BF16), kk.astype(BF16), nt_dims,
                                      preferred_element_type=F32), 0.0)
        for j in range(REC_LEVELS):
            m = 1 << j
            ql = (qf * jnp.exp(fwd[j])).astype(BF16)
            kl = (kk * jnp.exp(rev[j] - lf)).astype(BF16)
            al = lax.dot_general(ql, kl, nt_dims, preferred_element_type=F32)
            pair = ((ti >> (j + 1)) == (si >> (j + 1))) & ((ti & m) != 0) & ((si & m) == 0)
            a = jnp.where(pair, al, a)

        state_t = state_ref[...]
        o = (lax.dot_general((qf * jnp.exp(b)).astype(BF16), state_t.astype(BF16),
                             nt_dims, preferred_element_type=F32)
             + jnp.dot(a.astype(BF16), v, preferred_element_type=F32))
        b_last = b[c - 1:c, :]
        kd = (kk * jnp.exp(b_last - b)).astype(BF16)
        upd_t = lax.dot_general(v, kd, (((0,), (0,)), ((), ())),
                                preferred_element_type=F32)
        state_ref[...] = jnp.exp(b_last) * state_t + upd_t

        o = o * lax.rsqrt(jnp.mean(o * o, axis=-1, keepdims=True) + EPS) * gw_ref[...]
        o_ref[rows, :] = (o * g_ref[rows, :]).astype(o_ref.dtype)
        return carry

    lax.fori_loop(0, REC_ROWS // c, chunk, 0)


def _recurrence(proj, gnorm_w):
    s = proj.shape[0]
    blk = (REC_ROWS, HEAD_DIM)
    seg = lambda k: (lambda h, i: (i, k * N_HEADS + h))
    return pl.pallas_call(
        _rec_kernel,
        out_shape=jax.ShapeDtypeStruct((s, WIDTH), BF16),
        grid=(N_HEADS, s // REC_ROWS),
        in_specs=[pl.BlockSpec(blk, seg(3)),
                  pl.BlockSpec(blk, seg(4)),
                  pl.BlockSpec(blk, seg(5)),
                  pl.BlockSpec(blk, seg(6)),
                  pl.BlockSpec((1, HEAD_DIM), lambda h, i: (0, 0))],
        out_specs=pl.BlockSpec(blk, lambda h, i: (i, h)),
        scratch_shapes=[pltpu.VMEM((HEAD_DIM, HEAD_DIM), F32)],
        compiler_params=_params("parallel", "arbitrary"),
        name="hgrn",
    )(proj, proj, proj, proj, gnorm_w)


def _out_kernel(x_ref, ya_ref, yr_ref, w_ref, g_ref, nw_ref, sh_ref, sc_ref,
                x1_ref, h2_ref):
    y = (jnp.dot(ya_ref[...], w_ref[0:WIDTH, :], preferred_element_type=F32)
         + jnp.dot(yr_ref[...], w_ref[WIDTH:, :], preferred_element_type=F32))
    x1 = x_ref[...] + g_ref[...] * y
    x1_ref[...] = x1
    h2_ref[...] = _norm_modulate(x1, nw_ref[...], sc_ref[...], sh_ref[...]).astype(BF16)


def _out_proj(x2d, y_attn, y_rec, w_out_bf16, mod, norm2_w, tm=256):
    s, d = x2d.shape
    vec = lambda k: pl.BlockSpec((1, d), lambda i: (0, k))
    return pl.pallas_call(
        _out_kernel,
        out_shape=(jax.ShapeDtypeStruct((s, d), F32),
                   jax.ShapeDtypeStruct((s, d), BF16)),
        grid=(s // tm,),
        in_specs=[pl.BlockSpec((tm, d), lambda i: (i, 0)),
                  pl.BlockSpec((tm, WIDTH), lambda i: (i, 0)),
                  pl.BlockSpec((tm, WIDTH), lambda i: (i, 0)),
                  pl.BlockSpec((2 * WIDTH, d), lambda i: (0, 0)),
                  vec(2),
                  pl.BlockSpec((1, d), lambda i: (0, 0)),
                  vec(3), vec(4)],
        out_specs=(pl.BlockSpec((tm, d), lambda i: (i, 0)),
                   pl.BlockSpec((tm, d), lambda i: (i, 0))),
        compiler_params=_params("parallel"),
        name="out_proj",
    )(x2d, y_attn, y_rec, w_out_bf16, mod, norm2_w, mod, mod)


def _conv(u, tail_ref, cw_ref, cb_ref):
    tm = u.shape[0]
    w0, w1, w2 = cw_ref[0:1, :], cw_ref[1:2, :], cw_ref[2:3, :]
    body = u * w2 + pltpu.roll(u, 1, 0) * w1 + pltpu.roll(u, 2, 0) * w0
    cat = jnp.concatenate([tail_ref[...], u[0:SUBLANES]], axis=0)
    head = (u[0:SUBLANES] * w2
            + pltpu.roll(cat, 1, 0)[SUBLANES:] * w1
            + pltpu.roll(cat, 2, 0)[SUBLANES:] * w0)
    tail_ref[...] = u[tm - SUBLANES:]
    return jnp.concatenate([head, body[SUBLANES:]], axis=0) + cb_ref[...]


def _up_kernel(h_ref, wg_ref, wu_ref, cwg_ref, cwu_ref, cbg_ref, cbu_ref, o_ref,
               tailg_ref, tailu_ref):
    @pl.when(pl.program_id(1) == 0)
    def _():
        tailg_ref[...] = jnp.zeros_like(tailg_ref)
        tailu_ref[...] = jnp.zeros_like(tailu_ref)

    h = h_ref[...]
    gate = _conv(jnp.dot(h, wg_ref[...], preferred_element_type=F32),
                 tailg_ref, cwg_ref, cbg_ref)
    up = _conv(jnp.dot(h, wu_ref[...], preferred_element_type=F32),
               tailu_ref, cwu_ref, cbu_ref)
    o_ref[...] = (_silu(gate) * up).astype(o_ref.dtype)


def _ffn_up(h2, w_up_bf16, conv_w, conv_b, tm=512, tn=512):
    s, d = h2.shape
    nj = D_FF // tn
    return pl.pallas_call(
        _up_kernel,
        out_shape=jax.ShapeDtypeStruct((s, D_FF), BF16),
        grid=(nj, s // tm),
        in_specs=[pl.BlockSpec((tm, d), lambda j, i: (i, 0)),
                  pl.BlockSpec((d, tn), lambda j, i: (0, j)),
                  pl.BlockSpec((d, tn), lambda j, i: (0, nj + j)),
                  pl.BlockSpec((CONV_WIDTH, tn), lambda j, i: (0, j)),
                  pl.BlockSpec((CONV_WIDTH, tn), lambda j, i: (0, nj + j)),
                  pl.BlockSpec((1, tn), lambda j, i: (0, j)),
                  pl.BlockSpec((1, tn), lambda j, i: (0, nj + j))],
        out_specs=pl.BlockSpec((tm, tn), lambda j, i: (i, j)),
        scratch_shapes=[pltpu.VMEM((SUBLANES, tn), F32),
                        pltpu.VMEM((SUBLANES, tn), F32)],
        compiler_params=_params("parallel", "arbitrary"),
        name="ffn_up",
    )(h2, w_up_bf16, w_up_bf16, conv_w, conv_w, conv_b, conv_b)


def _down_kernel(a_ref, w_ref, x1_ref, g_ref, fw_ref, o_ref, acc_ref):
    k = pl.program_id(1)

    @pl.when(k == 0)
    def _():
        acc_ref[...] = jnp.zeros_like(acc_ref)

    acc_ref[...] += jnp.dot(a_ref[...], w_ref[...], preferred_element_type=F32)

    @pl.when(k == pl.num_programs(1) - 1)
    def _():
        x2 = x1_ref[...] + g_ref[...] * acc_ref[...]
        ms = jnp.mean(x2 * x2, axis=-1, keepdims=True)
        o_ref[...] = (x2 * lax.rsqrt(ms + EPS)) * fw_ref[...]


def _ffn_down(act, w_down_bf16, x1, mod, final_w, tm=512, tk=512):
    s, d = x1.shape
    return pl.pallas_call(
        _down_kernel,
        out_shape=jax.ShapeDtypeStruct((s, d), F32),
        grid=(s // tm, D_FF // tk),
        in_specs=[pl.BlockSpec((tm, tk), lambda i, k: (i, k)),
                  pl.BlockSpec((tk, d), lambda i, k: (k, 0)),
                  pl.BlockSpec((tm, d), lambda i, k: (i, 0)),
                  pl.BlockSpec((1, d), lambda i, k: (0, 5)),
                  pl.BlockSpec((1, d), lambda i, k: (0, 0))],
        out_specs=pl.BlockSpec((tm, d), lambda i, k: (i, 0)),
        scratch_shapes=[pltpu.VMEM((tm, d), F32)],
        compiler_params=_params("parallel", "arbitrary"),
        name="ffn_down",
    )(act, w_down_bf16, x1, mod, final_w)


def _rotary_inv_freq():
    inv = 1.0 / (ROPE_THETA ** (jnp.arange(0, ROT_DIM, 2, dtype=F32) / ROT_DIM))
    pad = jnp.zeros((HEAD_DIM - ROT_DIM,), F32)
    return jnp.concatenate([inv, inv, pad]).reshape(1, HEAD_DIM)


def kernel(x, c, positions, w_ada, b_ada, norm1_w, w_in, lb_logits, gnorm_w, w_out,
           norm2_w, w_up, conv_w, conv_b, w_down, final_norm_w):
    b, s, d = x.shape
    assert b == 1 and d == D_MODEL and w_ada.shape[0] == 1
    x2d = x.reshape(s, d)
    pos_col = positions.reshape(s, 1)

    mod = _modulation(c.reshape(d, 1), w_ada[0], b_ada)
    proj = _in_proj(x2d, pos_col, _rotary_inv_freq(), norm1_w, mod, mod, lb_logits,
                    w_in[0].astype(BF16))
    y_attn = _attention(proj)
    y_rec = _recurrence(proj, gnorm_w)
    x1, h2 = _out_proj(x2d, y_attn, y_rec, w_out[0].astype(BF16), mod, norm2_w)
    act = _ffn_up(h2, w_up[0].astype(BF16), conv_w[0], conv_b)
    out = _ffn_down(act, w_down[0].astype(BF16), x1, mod, final_norm_w.reshape(1, d))
    return out.reshape(b, s, d)
```

```python
import functools

import jax
import jax.numpy as jnp
from jax import lax
from jax.experimental import pallas as pl
from jax.experimental.pallas import tpu as pltpu

F32 = jnp.float32
BF16 = jnp.bfloat16

D_MODEL = 2048
N_HEADS = 8
HEAD_DIM = 128
WIDTH = N_HEADS * HEAD_DIM
N_SEG = 7
ROT_DIM = HEAD_DIM // 4
ROPE_THETA = 500000.0
DILATIONS = (1, 4, 16)
WINDOW_STEPS = 128
D_FF = 5632
CONV_WIDTH = 3
N_MOD = 6
EPS = 1e-6
MASKED = -1e30

VMEM_LIMIT = 56 * 1024 * 1024


def _sigmoid(x):
    return 1.0 / (1.0 + jnp.exp(-x))


def _silu(x):
    return x * _sigmoid(x)


def _params(*semantics):
    return pltpu.CompilerParams(dimension_semantics=semantics,
                                vmem_limit_bytes=VMEM_LIMIT)


def _mod_kernel(c_ref, w_ref, b_ref, o_ref):
    ca = _silu(c_ref[...])
    o_ref[...] = jnp.sum(w_ref[...] * ca, axis=0, keepdims=True) + b_ref[...]


def _modulation(c_col, w_ada, b_ada):
    d, n = w_ada.shape
    tn = 512
    return pl.pallas_call(
        _mod_kernel,
        out_shape=jax.ShapeDtypeStruct((1, n), F32),
        grid=(n // tn,),
        in_specs=[pl.BlockSpec((d, 1), lambda j: (0, 0)),
                  pl.BlockSpec((d, tn), lambda j: (0, j)),
                  pl.BlockSpec((1, tn), lambda j: (0, j))],
        out_specs=pl.BlockSpec((1, tn), lambda j: (0, j)),
        compiler_params=_params("parallel"),
        name="mod",
    )(c_col, w_ada, b_ada)


def _norm_modulate(x, norm_w, scale, shift):
    ms = jnp.mean(x * x, axis=-1, keepdims=True)
    return (x * lax.rsqrt(ms + EPS)) * norm_w * (1.0 + scale) + shift


def _in_kernel(x_ref, pos_ref, invf_ref, nw_ref, sh_ref, sc_ref, lbl_ref, w_ref,
               o_ref, h_ref, cos_ref, sina_ref, sinb_ref):
    j = pl.program_id(1)

    @pl.when(j == 0)
    def _():
        h = _norm_modulate(x_ref[...], nw_ref[...], sc_ref[...], sh_ref[...])
        h_ref[...] = h.astype(BF16)
        ang = pos_ref[...].astype(F32) * invf_ref[...]
        lane = lax.broadcasted_iota(jnp.int32, ang.shape, 1)
        c = jnp.cos(ang)
        s = jnp.sin(ang)
        half = ROT_DIM // 2
        cos_ref[...] = jnp.where(lane < ROT_DIM, c, 1.0)
        sina_ref[...] = jnp.where(lane < half, -s, 0.0)
        sinb_ref[...] = jnp.where((lane >= half) & (lane < ROT_DIM), s, 0.0)

    r = jnp.dot(h_ref[...], w_ref[...], preferred_element_type=F32)

    def rotary(scale):
        half = ROT_DIM // 2
        for hh in range(N_HEADS):
            sl = slice(hh * HEAD_DIM, (hh + 1) * HEAD_DIM)
            rh = r[:, sl]
            out = (rh * cos_ref[...]
                   + pltpu.roll(rh, HEAD_DIM - half, 1) * sina_ref[...]
                   + pltpu.roll(rh, half, 1) * sinb_ref[...])
            o_ref[:, sl] = out * scale

    @pl.when(j == 0)
    def _():
        rotary(HEAD_DIM ** -0.5)

    @pl.when(j == 1)
    def _():
        rotary(1.0)

    @pl.when((j == 2) | (j == 5))
    def _():
        o_ref[...] = r

    @pl.when(j == 3)
    def _():
        o_ref[...] = _silu(r) * (HEAD_DIM ** -0.5)

    @pl.when(j == 4)
    def _():
        lg = lbl_ref[...]
        e = jnp.exp(lg - jnp.max(lg, axis=0, keepdims=True))
        lb = e[0:1, :] / jnp.sum(e, axis=0, keepdims=True)
        f = lb + (1.0 - lb) * _sigmoid(r)
        o_ref[...] = jnp.log(f)

    @pl.when(j == 6)
    def _():
        o_ref[...] = _silu(r)


def _in_proj(x2d, pos_col, invf, norm_w, shift, scale, lb_logits, w_in_bf16, tm=512):
    s, d = x2d.shape
    n = w_in_bf16.shape[1]
    return pl.pallas_call(
        _in_kernel,
        out_shape=jax.ShapeDtypeStruct((s, n), F32),
        grid=(s // tm, n // WIDTH),
        in_specs=[pl.BlockSpec((tm, d), lambda i, j: (i, 0)),
                  pl.BlockSpec((tm, 1), lambda i, j: (i, 0)),
                  pl.BlockSpec((1, HEAD_DIM), lambda i, j: (0, 0)),
                  pl.BlockSpec((1, d), lambda i, j: (0, 0)),
                  pl.BlockSpec((1, d), lambda i, j: (0, 0)),
                  pl.BlockSpec((1, d), lambda i, j: (0, 1)),
                  pl.BlockSpec(lb_logits.shape, lambda i, j: (0, 0)),
                  pl.BlockSpec((d, WIDTH), lambda i, j: (0, j))],
        out_specs=pl.BlockSpec((tm, WIDTH), lambda i, j: (i, j)),
        scratch_shapes=[pltpu.VMEM((tm, d), BF16),
                        pltpu.VMEM((tm, HEAD_DIM), F32),
                        pltpu.VMEM((tm, HEAD_DIM), F32),
                        pltpu.VMEM((tm, HEAD_DIM), F32)],
        compiler_params=_params("parallel", "arbitrary"),
        name="in_proj",
    )(x2d, pos_col, invf, norm_w, shift, scale, lb_logits, w_in_bf16)


ATT_TILE = 2048
ATT_BLK = 128
ATT_ITERS = ATT_TILE // ATT_BLK


def _attn_kernel(q_ref, kp_ref, kc_ref, vp_ref, vc_ref, o_ref, acc_ref, m_ref, l_ref):
    qi = lax.broadcasted_iota(jnp.int32, (ATT_BLK, 2 * ATT_BLK), 0)
    ki = lax.broadcasted_iota(jnp.int32, (ATT_BLK, 2 * ATT_BLK), 1)
    band = (ki >= qi) & (ki <= qi + WINDOW_STEPS)
    band_first = band & ((ki >= ATT_BLK) | (pl.program_id(1) > 0))

    def rows(start, n, d):
        return pl.ds(start, n) if d == 1 else pl.ds(start, n, stride=d)

    for p, d in enumerate(DILATIONS):
        span = ATT_BLK * d
        for r in range(d):
            for blk in range(ATT_TILE // span):
                start = r + blk * span
                qb = q_ref[rows(start, ATT_BLK, d), :].astype(BF16)
                if blk > 0:
                    kk = kc_ref[rows(start - span, 2 * ATT_BLK, d), :].astype(BF16)
                    vv = vc_ref[rows(start - span, 2 * ATT_BLK, d), :].astype(BF16)
                    valid = band
                else:
                    prev = rows(ATT_TILE + start - span, ATT_BLK, d)
                    cur = rows(start, ATT_BLK, d)
                    kk = jnp.concatenate([kp_ref[prev, :].astype(BF16),
                                          kc_ref[cur, :].astype(BF16)], axis=0)
                    vv = jnp.concatenate([vp_ref[prev, :].astype(BF16),
                                          vc_ref[cur, :].astype(BF16)], axis=0)
                    valid = band_first
                s = lax.dot_general(qb, kk, (((1,), (1,)), ((), ())),
                                    preferred_element_type=F32)
                s = jnp.where(valid, s, MASKED)
                m = jnp.max(s, axis=1, keepdims=True)
                pe = jnp.exp(s - m)
                l = jnp.sum(pe, axis=1, keepdims=True)
                acc = jnp.dot(pe.astype(BF16), vv, preferred_element_type=F32)
                out_rows = rows(start, ATT_BLK, d)
                acc_ref[p, out_rows, :] = acc
                m_ref[p, out_rows, :] = jnp.broadcast_to(m, (ATT_BLK, HEAD_DIM))
                l_ref[p, out_rows, :] = jnp.broadcast_to(l, (ATT_BLK, HEAD_DIM))

    m_all = jnp.maximum(jnp.maximum(m_ref[0], m_ref[1]), m_ref[2])
    num = jnp.zeros((ATT_TILE, HEAD_DIM), F32)
    den = jnp.zeros((ATT_TILE, HEAD_DIM), F32)
    for p in range(len(DILATIONS)):
        w = jnp.exp(m_ref[p] - m_all)
        num = num + w * acc_ref[p]
        den = den + w * l_ref[p]
    o_ref[...] = (num / den).astype(o_ref.dtype)


def _attention(proj):
    s = proj.shape[0]
    nt = s // ATT_TILE
    blk = (ATT_TILE, HEAD_DIM)
    prev = lambda t: jnp.maximum(t - 1, 0)
    return pl.pallas_call(
        _attn_kernel,
        out_shape=jax.ShapeDtypeStruct((s, WIDTH), BF16),
        grid=(N_HEADS, nt),
        in_specs=[pl.BlockSpec(blk, lambda h, t: (t, h)),
                  pl.BlockSpec(blk, lambda h, t: (prev(t), N_HEADS + h)),
                  pl.BlockSpec(blk, lambda h, t: (t, N_HEADS + h)),
                  pl.BlockSpec(blk, lambda h, t: (prev(t), 2 * N_HEADS + h)),
                  pl.BlockSpec(blk, lambda h, t: (t, 2 * N_HEADS + h))],
        out_specs=pl.BlockSpec(blk, lambda h, t: (t, h)),
        scratch_shapes=[pltpu.VMEM((len(DILATIONS), ATT_TILE, HEAD_DIM), F32),
                        pltpu.VMEM((len(DILATIONS), ATT_TILE, HEAD_DIM), F32),
                        pltpu.VMEM((len(DILATIONS), ATT_TILE, HEAD_DIM), F32)],
        compiler_params=_params("parallel", "parallel"),
        name="attn",
    )(proj, proj, proj, proj, proj)


REC_CHUNK = 128
REC_LEVELS = 7
REC_ROWS = 1024
SUBLANES = 8


def _block_scans(lf):
    c = lf.shape[0]
    row = lax.broadcasted_iota(jnp.int32, (c, HEAD_DIM), 0)
    fwd, rev = [lf], [lf]
    x, y = lf, lf
    for j in range(REC_LEVELS):
        m = 1 << j
        if m < SUBLANES:
            x3 = x.reshape(c // SUBLANES, SUBLANES, HEAD_DIM)
            y3 = y.reshape(c // SUBLANES, SUBLANES, HEAD_DIM)
            addx = jnp.zeros_like(x)
            addy = jnp.zeros_like(y)
            for k in range(m):
                sx = pltpu.roll(x3, k + 1, 1).reshape(c, HEAD_DIM)
                addx = jnp.where((row & (2 * m - 1)) == m + k, sx, addx)
                sy = pltpu.roll(y3, SUBLANES - (k + 1), 1).reshape(c, HEAD_DIM)
                addy = jnp.where((row & (2 * m - 1)) == m - 1 - k, sy, addy)
            x = x + addx
            y = y + addy
        else:
            xs, ys = [], []
            for b in range(c // (2 * m)):
                lo = x[b * 2 * m: b * 2 * m + m]
                hi = x[b * 2 * m + m: (b + 1) * 2 * m]
                xs += [lo, hi + lo[m - 1:m, :]]
                lo = y[b * 2 * m: b * 2 * m + m]
                hi = y[b * 2 * m + m: (b + 1) * 2 * m]
                ys += [lo + hi[0:1, :], hi]
            x = jnp.concatenate(xs, axis=0)
            y = jnp.concatenate(ys, axis=0)
        fwd.append(x)
        rev.append(y)
    return fwd, rev


def _rec_kernel(q_ref, lf_ref, v_ref, g_ref, gw_ref, o_ref, state_ref):
    @pl.when(pl.program_id(1) == 0)
    def _():
        state_ref[...] = jnp.zeros_like(state_ref)

    c = REC_CHUNK
    ti = lax.broadcasted_iota(jnp.int32, (c, c), 0)
    si = lax.broadcasted_iota(jnp.int32, (c, c), 1)
    nt_dims = (((1,), (1,)), ((), ()))

    def chunk(ci, carry):
        rows = pl.ds(pl.multiple_of(ci * c, c), c)
        qf = q_ref[rows, :]
        lf = lf_ref[rows, :]
        v = v_ref[rows, :].astype(BF16)
        kk = 1.0 - jnp.exp(lf)
        fwd, rev = _block_scans(lf)
        b = fwd[REC_LEVELS]

        a = jnp.where(ti == si,
                      lax.dot_general(qf.astype(BF16), kk.astype(BF16), nt_dims,
                                      preferred_element_type=F32), 0.0)
        for j in range(REC_LEVELS):
            m = 1 << j
            ql = (qf * jnp.exp(fwd[j])).astype(BF16)
            kl = (kk * jnp.exp(rev[j] - lf)).astype(BF16)
            al = lax.dot_general(ql, kl, nt_dims, preferred_element_type=F32)
            pair = ((ti >> (j + 1)) == (si >> (j + 1))) & ((ti & m) != 0) & ((si & m) == 0)
            a = jnp.where(pair, al, a)

        state_t = state_ref[...]
        o = (lax.dot_general((qf * jnp.exp(b)).astype(BF16), state_t.astype(BF16),
                             nt_dims, preferred_element_type=F32)
             + jnp.dot(a.astype(BF16), v, preferred_element_type=F32))
        b_last = b[c - 1:c, :]
        kd = (kk * jnp.exp(b_last - b)).astype(BF16)
        upd_t = lax.dot_general(v, kd, (((0,), (0,)), ((), ())),
                                preferred_element_type=F32)
        state_ref[...] = jnp.exp(b_last) * state_t + upd_t

        o = o * lax.rsqrt(jnp.mean(o * o, axis=-1, keepdims=True) + EPS) * gw_ref[...]
        o_ref[rows, :] = (o * g_ref[rows, :]).astype(o_ref.dtype)
        return carry

    lax.fori_loop(0, REC_ROWS // c, chunk, 0)


def _recurrence(proj, gnorm_w):
    s = proj.shape[0]
    blk = (REC_ROWS, HEAD_DIM)
    seg = lambda k: (lambda h, i: (i, k * N_HEADS + h))
    return pl.pallas_call(
        _rec_kernel,
        out_shape=jax.ShapeDtypeStruct((s, WIDTH), BF16),
        grid=(N_HEADS, s // REC_ROWS),
        in_specs=[pl.BlockSpec(blk, seg(3)),
                  pl.BlockSpec(blk, seg(4)),
                  pl.BlockSpec(blk, seg(5)),
                  pl.BlockSpec(blk, seg(6)),
                  pl.BlockSpec((1, HEAD_DIM), lambda h, i: (0, 0))],
        out_specs=pl.BlockSpec(blk, lambda h, i: (i, h)),
        scratch_shapes=[pltpu.VMEM((HEAD_DIM, HEAD_DIM), F32)],
        compiler_params=_params("parallel", "arbitrary"),
        name="hgrn",
    )(proj, proj, proj, proj, gnorm_w)


def _out_kernel(x_ref, ya_ref, yr_ref, w_ref, g_ref, nw_ref, sh_ref, sc_ref,
                x1_ref, h2_ref):
    y = (jnp.dot(ya_ref[...], w_ref[0:WIDTH, :], preferred_element_type=F32)
         + jnp.dot(yr_ref[...], w_ref[WIDTH:, :], preferred_element_type=F32))
    x1 = x_ref[...] + g_ref[...] * y
    x1_ref[...] = x1
    h2_ref[...] = _norm_modulate(x1, nw_ref[...], sc_ref[...], sh_ref[...]).astype(BF16)


def _out_proj(x2d, y_attn, y_rec, w_out_bf16, mod, norm2_w, tm=256):
    s, d = x2d.shape
    vec = lambda k: pl.BlockSpec((1, d), lambda i: (0, k))
    return pl.pallas_call(
        _out_kernel,
        out_shape=(jax.ShapeDtypeStruct((s, d), F32),
                   jax.ShapeDtypeStruct((s, d), BF16)),
        grid=(s // tm,),
        in_specs=[pl.BlockSpec((tm, d), lambda i: (i, 0)),
                  pl.BlockSpec((tm, WIDTH), lambda i: (i, 0)),
                  pl.BlockSpec((tm, WIDTH), lambda i: (i, 0)),
                  pl.BlockSpec((2 * WIDTH, d), lambda i: (0, 0)),
                  vec(2),
                  pl.BlockSpec((1, d), lambda i: (0, 0)),
                  vec(3), vec(4)],
        out_specs=(pl.BlockSpec((tm, d), lambda i: (i, 0)),
                   pl.BlockSpec((tm, d), lambda i: (i, 0))),
        compiler_params=_params("parallel"),
        name="out_proj",
    )(x2d, y_attn, y_rec, w_out_bf16, mod, norm2_w, mod, mod)


def _conv(u, tail_ref, cw_ref, cb_ref):
    tm = u.shape[0]
    w0, w1, w2 = cw_ref[0:1, :], cw_ref[1:2, :], cw_ref[2:3, :]
    body = u * w2 + pltpu.roll(u, 1, 0) * w1 + pltpu.roll(u, 2, 0) * w0
    cat = jnp.concatenate([tail_ref[...], u[0:SUBLANES]], axis=0)
    head = (u[0:SUBLANES] * w2
            + pltpu.roll(cat, 1, 0)[SUBLANES:] * w1
            + pltpu.roll(cat, 2, 0)[SUBLANES:] * w0)
    tail_ref[...] = u[tm - SUBLANES:]
    return jnp.concatenate([head, body[SUBLANES:]], axis=0) + cb_ref[...]


def _up_kernel(h_ref, wg_ref, wu_ref, cwg_ref, cwu_ref, cbg_ref, cbu_ref, o_ref,
               tailg_ref, tailu_ref):
    @pl.when(pl.program_id(1) == 0)
    def _():
        tailg_ref[...] = jnp.zeros_like(tailg_ref)
        tailu_ref[...] = jnp.zeros_like(tailu_ref)

    h = h_ref[...]
    gate = _conv(jnp.dot(h, wg_ref[...], preferred_element_type=F32),
                 tailg_ref, cwg_ref, cbg_ref)
    up = _conv(jnp.dot(h, wu_ref[...], preferred_element_type=F32),
               tailu_ref, cwu_ref, cbu_ref)
    o_ref[...] = (_silu(gate) * up).astype(o_ref.dtype)


def _ffn_up(h2, w_up_bf16, conv_w, conv_b, tm=512, tn=512):
    s, d = h2.shape
    nj = D_FF // tn
    return pl.pallas_call(
        _up_kernel,
        out_shape=jax.ShapeDtypeStruct((s, D_FF), BF16),
        grid=(nj, s // tm),
        in_specs=[pl.BlockSpec((tm, d), lambda j, i: (i, 0)),
                  pl.BlockSpec((d, tn), lambda j, i: (0, j)),
                  pl.BlockSpec((d, tn), lambda j, i: (0, nj + j)),
                  pl.BlockSpec((CONV_WIDTH, tn), lambda j, i: (0, j)),
                  pl.BlockSpec((CONV_WIDTH, tn), lambda j, i: (0, nj + j)),
                  pl.BlockSpec((1, tn), lambda j, i: (0, j)),
                  pl.BlockSpec((1, tn), lambda j, i: (0, nj + j))],
        out_specs=pl.BlockSpec((tm, tn), lambda j, i: (i, j)),
        scratch_shapes=[pltpu.VMEM((SUBLANES, tn), F32),
                        pltpu.VMEM((SUBLANES, tn), F32)],
        compiler_params=_params("parallel", "arbitrary"),
        name="ffn_up",
    )(h2, w_up_bf16, w_up_bf16, conv_w, conv_w, conv_b, conv_b)


def _down_kernel(a_ref, w_ref, x1_ref, g_ref, fw_ref, o_ref, acc_ref):
    k = pl.program_id(1)

    @pl.when(k == 0)
    def _():
        acc_ref[...] = jnp.zeros_like(acc_ref)

    acc_ref[...] += jnp.dot(a_ref[...], w_ref[...], preferred_element_type=F32)

    @pl.when(k == pl.num_programs(1) - 1)
    def _():
        x2 = x1_ref[...] + g_ref[...] * acc_ref[...]
        ms = jnp.mean(x2 * x2, axis=-1, keepdims=True)
        o_ref[...] = (x2 * lax.rsqrt(ms + EPS)) * fw_ref[...]


def _ffn_down(act, w_down_bf16, x1, mod, final_w, tm=512, tk=512):
    s, d = x1.shape
    return pl.pallas_call(
        _down_kernel,
        out_shape=jax.ShapeDtypeStruct((s, d), F32),
        grid=(s // tm, D_FF // tk),
        in_specs=[pl.BlockSpec((tm, tk), lambda i, k: (i, k)),
                  pl.BlockSpec((tk, d), lambda i, k: (k, 0)),
                  pl.BlockSpec((tm, d), lambda i, k: (i, 0)),
                  pl.BlockSpec((1, d), lambda i, k: (0, 5)),
                  pl.BlockSpec((1, d), lambda i, k: (0, 0))],
        out_specs=pl.BlockSpec((tm, d), lambda i, k: (i, 0)),
        scratch_shapes=[pltpu.VMEM((tm, d), F32)],
        compiler_params=_params("parallel", "arbitrary"),
        name="ffn_down",
    )(act, w_down_bf16, x1, mod, final_w)


def _rotary_inv_freq():
    inv = 1.0 / (ROPE_THETA ** (jnp.arange(0, ROT_DIM, 2, dtype=F32) / ROT_DIM))
    pad = jnp.zeros((HEAD_DIM - ROT_DIM,), F32)
    return jnp.concatenate([inv, inv, pad]).reshape(1, HEAD_DIM)


def kernel(x, c, positions, w_ada, b_ada, norm1_w, w_in, lb_logits, gnorm_w, w_out,
           norm2_w, w_up, conv_w, conv_b, w_down, final_norm_w):
    b, s, d = x.shape
    assert b == 1 and d == D_MODEL and w_ada.shape[0] == 1
    x2d = x.reshape(s, d)
    pos_col = positions.reshape(s, 1)

    mod = _modulation(c.reshape(d, 1), w_ada[0], b_ada)
    proj = _in_proj(x2d, pos_col, _rotary_inv_freq(), norm1_w, mod, mod, lb_logits,
                    w_in[0].astype(BF16))
    y_attn = _attention(proj)
    y_rec = _recurrence(proj, gnorm_w)
    x1, h2 = _out_proj(x2d, y_attn, y_rec, w_out[0].astype(BF16), mod, norm2_w)
    act = _ffn_up(h2, w_up[0].astype(BF16), conv_w[0], conv_b)
    out = _ffn_down(act, w_down[0].astype(BF16), x1, mod, final_norm_w.reshape(1, d))
    return out.reshape(b, s, d)
```

```python
import functools

import jax
import jax.numpy as jnp
from jax import lax
from jax.experimental import pallas as pl
from jax.experimental.pallas import tpu as pltpu

F32 = jnp.float32
BF16 = jnp.bfloat16

D_MODEL = 2048
N_HEADS = 8
HEAD_DIM = 128
WIDTH = N_HEADS * HEAD_DIM
N_SEG = 7
ROT_DIM = HEAD_DIM // 4
ROPE_THETA = 500000.0
DILATIONS = (1, 4, 16)
WINDOW_STEPS = 128
D_FF = 5632
CONV_WIDTH = 3
N_MOD = 6
EPS = 1e-6
MASKED = -1e30

VMEM_LIMIT = 56 * 1024 * 1024
SUB_N = 256


def _sigmoid(x):
    return 1.0 / (1.0 + jnp.exp(-x))


def _silu(x):
    return x * _sigmoid(x)


def _params(*semantics):
    return pltpu.CompilerParams(dimension_semantics=semantics,
                                vmem_limit_bytes=VMEM_LIMIT)


def _mod_kernel(c_ref, w_ref, b_ref, o_ref):
    ca = _silu(c_ref[...])
    o_ref[...] = jnp.sum(w_ref[...] * ca, axis=0, keepdims=True) + b_ref[...]


def _modulation(c_col, w_ada, b_ada):
    d, n = w_ada.shape
    tn = 512
    return pl.pallas_call(
        _mod_kernel,
        out_shape=jax.ShapeDtypeStruct((1, n), F32),
        grid=(n // tn,),
        in_specs=[pl.BlockSpec((d, 1), lambda j: (0, 0)),
                  pl.BlockSpec((d, tn), lambda j: (0, j)),
                  pl.BlockSpec((1, tn), lambda j: (0, j))],
        out_specs=pl.BlockSpec((1, tn), lambda j: (0, j)),
        compiler_params=_params("parallel"),
        name="mod",
    )(c_col, w_ada, b_ada)


def _norm_modulate(x, norm_w, scale, shift):
    ms = jnp.mean(x * x, axis=-1, keepdims=True)
    return (x * lax.rsqrt(ms + EPS)) * norm_w * (1.0 + scale) + shift


def _in_kernel(x_ref, pos_ref, invf_ref, nw_ref, sh_ref, sc_ref, lbl_ref, w_ref,
               o_ref, h_ref, cos_ref, sina_ref, sinb_ref):
    j = pl.program_id(1)

    @pl.when(j == 0)
    def _():
        h = _norm_modulate(x_ref[...], nw_ref[...], sc_ref[...], sh_ref[...])
        h_ref[...] = h.astype(BF16)
        ang = pos_ref[...].astype(F32) * invf_ref[...]
        lane = lax.broadcasted_iota(jnp.int32, ang.shape, 1)
        c = jnp.cos(ang)
        s = jnp.sin(ang)
        half = ROT_DIM // 2
        cos_ref[...] = jnp.where(lane < ROT_DIM, c, 1.0)
        sina_ref[...] = jnp.where(lane < half, -s, 0.0)
        sinb_ref[...] = jnp.where((lane >= half) & (lane < ROT_DIM), s, 0.0)

    def for_subtiles(epilogue):
        for c0 in range(0, WIDTH, SUB_N):
            cols = slice(c0, c0 + SUB_N)
            r = jnp.dot(h_ref[...], w_ref[:, cols], preferred_element_type=F32)
            o_ref[:, cols] = epilogue(r, cols)

    def rotary(scale):
        half = ROT_DIM // 2

        def epilogue(r, cols):
            outs = []
            for hh in range(SUB_N // HEAD_DIM):
                rh = r[:, hh * HEAD_DIM:(hh + 1) * HEAD_DIM]
                out = (rh * cos_ref[...]
                       + pltpu.roll(rh, HEAD_DIM - half, 1) * sina_ref[...]
                       + pltpu.roll(rh, half, 1) * sinb_ref[...])
                outs.append(out * scale if scale != 1.0 else out)
            return jnp.concatenate(outs, axis=1)
        return epilogue

    @pl.when(j == 0)
    def _():
        for_subtiles(rotary(HEAD_DIM ** -0.5))

    @pl.when(j == 1)
    def _():
        for_subtiles(rotary(1.0))

    @pl.when((j == 2) | (j == 5))
    def _():
        for_subtiles(lambda r, cols: r)

    @pl.when(j == 3)
    def _():
        for_subtiles(lambda r, cols: _silu(r) * (HEAD_DIM ** -0.5))

    @pl.when(j == 4)
    def _():
        lg = lbl_ref[...]
        e = jnp.exp(lg - jnp.max(lg, axis=0, keepdims=True))
        lb = e[0:1, :] / jnp.sum(e, axis=0, keepdims=True)

        def epilogue(r, cols):
            f = lb[:, cols] + (1.0 - lb[:, cols]) * _sigmoid(r)
            return jnp.log(f)
        for_subtiles(epilogue)

    @pl.when(j == 6)
    def _():
        for_subtiles(lambda r, cols: _silu(r))


def _in_proj(x2d, pos_col, invf, norm_w, shift, scale, lb_logits, w_in_bf16, tm=512):
    s, d = x2d.shape
    n = w_in_bf16.shape[1]
    return pl.pallas_call(
        _in_kernel,
        out_shape=jax.ShapeDtypeStruct((s, n), F32),
        grid=(s // tm, n // WIDTH),
        in_specs=[pl.BlockSpec((tm, d), lambda i, j: (i, 0)),
                  pl.BlockSpec((tm, 1), lambda i, j: (i, 0)),
                  pl.BlockSpec((1, HEAD_DIM), lambda i, j: (0, 0)),
                  pl.BlockSpec((1, d), lambda i, j: (0, 0)),
                  pl.BlockSpec((1, d), lambda i, j: (0, 0)),
                  pl.BlockSpec((1, d), lambda i, j: (0, 1)),
                  pl.BlockSpec(lb_logits.shape, lambda i, j: (0, 0)),
                  pl.BlockSpec((d, WIDTH), lambda i, j: (0, j))],
        out_specs=pl.BlockSpec((tm, WIDTH), lambda i, j: (i, j)),
        scratch_shapes=[pltpu.VMEM((tm, d), BF16),
                        pltpu.VMEM((tm, HEAD_DIM), F32),
                        pltpu.VMEM((tm, HEAD_DIM), F32),
                        pltpu.VMEM((tm, HEAD_DIM), F32)],
        compiler_params=_params("parallel", "arbitrary"),
        name="in_proj",
    )(x2d, pos_col, invf, norm_w, shift, scale, lb_logits, w_in_bf16)


ATT_TILE = 2048
ATT_BLK = 128
ATT_ITERS = ATT_TILE // ATT_BLK


def _attn_kernel(q_ref, kp_ref, kc_ref, vp_ref, vc_ref, o_ref, acc_ref, m_ref, l_ref):
    qi = lax.broadcasted_iota(jnp.int32, (ATT_BLK, 2 * ATT_BLK), 0)
    ki = lax.broadcasted_iota(jnp.int32, (ATT_BLK, 2 * ATT_BLK), 1)
    band = (ki >= qi) & (ki <= qi + WINDOW_STEPS)
    band_first = band & ((ki >= ATT_BLK) | (pl.program_id(1) > 0))

    def rows(start, n, d):
        return pl.ds(start, n) if d == 1 else pl.ds(start, n, stride=d)

    for p, d in enumerate(DILATIONS):
        span = ATT_BLK * d
        for r in range(d):
            for blk in range(ATT_TILE // span):
                start = r + blk * span
                qb = q_ref[rows(start, ATT_BLK, d), :].astype(BF16)
                if blk > 0:
                    kk = kc_ref[rows(start - span, 2 * ATT_BLK, d), :].astype(BF16)
                    vv = vc_ref[rows(start - span, 2 * ATT_BLK, d), :].astype(BF16)
                    valid = band
                else:
                    prev = rows(ATT_TILE + start - span, ATT_BLK, d)
                    cur = rows(start, ATT_BLK, d)
                    kk = jnp.concatenate([kp_ref[prev, :].astype(BF16),
                                          kc_ref[cur, :].astype(BF16)], axis=0)
                    vv = jnp.concatenate([vp_ref[prev, :].astype(BF16),
                                          vc_ref[cur, :].astype(BF16)], axis=0)
                    valid = band_first
                s = lax.dot_general(qb, kk, (((1,), (1,)), ((), ())),
                                    preferred_element_type=F32)
                s = jnp.where(valid, s, MASKED)
                m = jnp.max(s, axis=1, keepdims=True)
                pe = jnp.exp(s - m)
                l = jnp.sum(pe, axis=1, keepdims=True)
                acc = jnp.dot(pe.astype(BF16), vv, preferred_element_type=F32)
                out_rows = rows(start, ATT_BLK, d)
                acc_ref[p, out_rows, :] = acc
                m_ref[p, out_rows, :] = jnp.broadcast_to(m, (ATT_BLK, HEAD_DIM))
                l_ref[p, out_rows, :] = jnp.broadcast_to(l, (ATT_BLK, HEAD_DIM))

    m_all = jnp.maximum(jnp.maximum(m_ref[0], m_ref[1]), m_ref[2])
    num = jnp.zeros((ATT_TILE, HEAD_DIM), F32)
    den = jnp.zeros((ATT_TILE, HEAD_DIM), F32)
    for p in range(len(DILATIONS)):
        w = jnp.exp(m_ref[p] - m_all)
        num = num + w * acc_ref[p]
        den = den + w * l_ref[p]
    o_ref[...] = (num / den).astype(o_ref.dtype)


def _attention(proj):
    s = proj.shape[0]
    nt = s // ATT_TILE
    blk = (ATT_TILE, HEAD_DIM)
    prev = lambda t: jnp.maximum(t - 1, 0)
    return pl.pallas_call(
        _attn_kernel,
        out_shape=jax.ShapeDtypeStruct((s, WIDTH), BF16),
        grid=(N_HEADS, nt),
        in_specs=[pl.BlockSpec(blk, lambda h, t: (t, h)),
                  pl.BlockSpec(blk, lambda h, t: (prev(t), N_HEADS + h)),
                  pl.BlockSpec(blk, lambda h, t: (t, N_HEADS + h)),
                  pl.BlockSpec(blk, lambda h, t: (prev(t), 2 * N_HEADS + h)),
                  pl.BlockSpec(blk, lambda h, t: (t, 2 * N_HEADS + h))],
        out_specs=pl.BlockSpec(blk, lambda h, t: (t, h)),
        scratch_shapes=[pltpu.VMEM((len(DILATIONS), ATT_TILE, HEAD_DIM), F32),
                        pltpu.VMEM((len(DILATIONS), ATT_TILE, HEAD_DIM), F32),
                        pltpu.VMEM((len(DILATIONS), ATT_TILE, HEAD_DIM), F32)],
        compiler_params=_params("parallel", "parallel"),
        name="attn",
    )(proj, proj, proj, proj, proj)


REC_CHUNK = 128
REC_LEVELS = 7
REC_ROWS = 1024
REC_HEADS = 4
SUBLANES = 8


def _block_scans(lf):
    c = lf.shape[0]
    row = lax.broadcasted_iota(jnp.int32, (c, HEAD_DIM), 0)
    fwd, rev = [lf], [lf]
    x, y = lf, lf
    for j in range(REC_LEVELS):
        m = 1 << j
        if m < SUBLANES:
            x3 = x.reshape(c // SUBLANES, SUBLANES, HEAD_DIM)
            y3 = y.reshape(c // SUBLANES, SUBLANES, HEAD_DIM)
            addx = jnp.zeros_like(x)
            addy = jnp.zeros_like(y)
            for k in range(m):
                sx = pltpu.roll(x3, k + 1, 1).reshape(c, HEAD_DIM)
                addx = jnp.where((row & (2 * m - 1)) == m + k, sx, addx)
                sy = pltpu.roll(y3, SUBLANES - (k + 1), 1).reshape(c, HEAD_DIM)
                addy = jnp.where((row & (2 * m - 1)) == m - 1 - k, sy, addy)
            x = x + addx
            y = y + addy
        else:
            xs, ys = [], []
            for b in range(c // (2 * m)):
                lo = x[b * 2 * m: b * 2 * m + m]
                hi = x[b * 2 * m + m: (b + 1) * 2 * m]
                xs += [lo, hi + lo[m - 1:m, :]]
                lo = y[b * 2 * m: b * 2 * m + m]
                hi = y[b * 2 * m + m: (b + 1) * 2 * m]
                ys += [lo + hi[0:1, :], hi]
            x = jnp.concatenate(xs, axis=0)
            y = jnp.concatenate(ys, axis=0)
        fwd.append(x)
        rev.append(y)
    return fwd, rev


def _rec_kernel(q_ref, lf_ref, v_ref, g_ref, gw_ref, o_ref, state_ref):
    @pl.when(pl.program_id(1) == 0)
    def _():
        state_ref[...] = jnp.zeros_like(state_ref)

    c = REC_CHUNK
    ti = lax.broadcasted_iota(jnp.int32, (c, c), 0)
    si = lax.broadcasted_iota(jnp.int32, (c, c), 1)
    diff = ti ^ si
    level = jnp.full((c, c), -1, jnp.int32)
    for j in range(REC_LEVELS):
        level = jnp.where((diff >> j) == 1, j, level)
    level = jnp.where(si > ti, -1, jnp.where(si == ti, REC_LEVELS, level))
    nt_dims = (((1,), (1,)), ((), ()))

    def one_head(hh, rows):
        lanes = slice(hh * HEAD_DIM, (hh + 1) * HEAD_DIM)
        qf = q_ref[rows, lanes]
        lf = lf_ref[rows, lanes]
        v = v_ref[rows, lanes].astype(BF16)
        kk = 1.0 - jnp.exp(lf)
        fwd, rev = _block_scans(lf)
        b = fwd[REC_LEVELS]

        a = jnp.where(level == REC_LEVELS,
                      lax.dot_general(qf.astype(BF16), kk.astype(BF16), nt_dims,
                                      preferred_element_type=F32), 0.0)
        for j in range(REC_LEVELS):
            ql = (qf * jnp.exp(fwd[j])).astype(BF16)
            kl = (kk * jnp.exp(rev[j] - lf)).astype(BF16)
            al = lax.dot_general(ql, kl, nt_dims, preferred_element_type=F32)
            a = jnp.where(level == j, al, a)

        state_t = state_ref[hh]
        o = (lax.dot_general((qf * jnp.exp(b)).astype(BF16), state_t.astype(BF16),
                             nt_dims, preferred_element_type=F32)
             + jnp.dot(a.astype(BF16), v, preferred_element_type=F32))
        b_last = b[c - 1:c, :]
        kd = (kk * jnp.exp(b_last - b)).astype(BF16)
        upd_t = lax.dot_general(v, kd, (((0,), (0,)), ((), ())),
                                preferred_element_type=F32)
        state_ref[hh] = jnp.exp(b_last) * state_t + upd_t

        o = o * lax.rsqrt(jnp.mean(o * o, axis=-1, keepdims=True) + EPS) * gw_ref[...]
        o_ref[rows, lanes] = (o * g_ref[rows, lanes]).astype(o_ref.dtype)

    def chunk(ci, carry):
        rows = pl.ds(pl.multiple_of(ci * c, c), c)
        for hh in range(REC_HEADS):
            one_head(hh, rows)
        return carry

    lax.fori_loop(0, REC_ROWS // c, chunk, 0)


def _recurrence(proj, gnorm_w):
    s = proj.shape[0]
    blk = (REC_ROWS, REC_HEADS * HEAD_DIM)
    groups = N_HEADS // REC_HEADS
    seg = lambda k: (lambda h, i: (i, k * groups + h))
    return pl.pallas_call(
        _rec_kernel,
        out_shape=jax.ShapeDtypeStruct((s, WIDTH), BF16),
        grid=(groups, s // REC_ROWS),
        in_specs=[pl.BlockSpec(blk, seg(3)),
                  pl.BlockSpec(blk, seg(4)),
                  pl.BlockSpec(blk, seg(5)),
                  pl.BlockSpec(blk, seg(6)),
                  pl.BlockSpec((1, HEAD_DIM), lambda h, i: (0, 0))],
        out_specs=pl.BlockSpec(blk, lambda h, i: (i, h)),
        scratch_shapes=[pltpu.VMEM((REC_HEADS, HEAD_DIM, HEAD_DIM), F32)],
        compiler_params=_params("parallel", "arbitrary"),
        name="hgrn",
    )(proj, proj, proj, proj, gnorm_w)


def _out_kernel(x_ref, ya_ref, yr_ref, w_ref, g_ref, nw_ref, sh_ref, sc_ref,
                x1_ref, h2_ref):
    y = (jnp.dot(ya_ref[...], w_ref[0:WIDTH, :], preferred_element_type=F32)
         + jnp.dot(yr_ref[...], w_ref[WIDTH:, :], preferred_element_type=F32))
    x1 = x_ref[...] + g_ref[...] * y
    x1_ref[...] = x1
    h2_ref[...] = _norm_modulate(x1, nw_ref[...], sc_ref[...], sh_ref[...]).astype(BF16)


def _out_proj(x2d, y_attn, y_rec, w_out_bf16, mod, norm2_w, tm=256):
    s, d = x2d.shape
    vec = lambda k: pl.BlockSpec((1, d), lambda i: (0, k))
    return pl.pallas_call(
        _out_kernel,
        out_shape=(jax.ShapeDtypeStruct((s, d), F32),
                   jax.ShapeDtypeStruct((s, d), BF16)),
        grid=(s // tm,),
        in_specs=[pl.BlockSpec((tm, d), lambda i: (i, 0)),
                  pl.BlockSpec((tm, WIDTH), lambda i: (i, 0)),
                  pl.BlockSpec((tm, WIDTH), lambda i: (i, 0)),
                  pl.BlockSpec((2 * WIDTH, d), lambda i: (0, 0)),
                  vec(2),
                  pl.BlockSpec((1, d), lambda i: (0, 0)),
                  vec(3), vec(4)],
        out_specs=(pl.BlockSpec((tm, d), lambda i: (i, 0)),
                   pl.BlockSpec((tm, d), lambda i: (i, 0))),
        compiler_params=_params("parallel"),
        name="out_proj",
    )(x2d, y_attn, y_rec, w_out_bf16, mod, norm2_w, mod, mod)


def _conv(u, tail_ref, cw_ref, cb_ref):
    tm = u.shape[0]
    w0, w1, w2 = cw_ref[0:1, :], cw_ref[1:2, :], cw_ref[2:3, :]
    body = u * w2 + pltpu.roll(u, 1, 0) * w1 + pltpu.roll(u, 2, 0) * w0
    cat = jnp.concatenate([tail_ref[...], u[0:SUBLANES]], axis=0)
    head = (u[0:SUBLANES] * w2
            + pltpu.roll(cat, 1, 0)[SUBLANES:] * w1
            + pltpu.roll(cat, 2, 0)[SUBLANES:] * w0)
    tail_ref[...] = u[tm - SUBLANES:]
    return jnp.concatenate([head, body[SUBLANES:]], axis=0) + cb_ref[...]


def _up_kernel(h_ref, wg_ref, wu_ref, cwg_ref, cwu_ref, cbg_ref, cbu_ref, o_ref,
               tailg_ref, tailu_ref):
    @pl.when(pl.program_id(1) == 0)
    def _():
        tailg_ref[...] = jnp.zeros_like(tailg_ref)
        tailu_ref[...] = jnp.zeros_like(tailu_ref)

    h = h_ref[...]
    gate = _conv(jnp.dot(h, wg_ref[...], preferred_element_type=F32),
                 tailg_ref, cwg_ref, cbg_ref)
    up = _conv(jnp.dot(h, wu_ref[...], preferred_element_type=F32),
               tailu_ref, cwu_ref, cbu_ref)
    o_ref[...] = (_silu(gate) * up).astype(o_ref.dtype)


def _ffn_up(h2, w_up_bf16, conv_w, conv_b, tm=512, tn=512):
    s, d = h2.shape
    nj = D_FF // tn
    return pl.pallas_call(
        _up_kernel,
        out_shape=jax.ShapeDtypeStruct((s, D_FF), BF16),
        grid=(nj, s // tm),
        in_specs=[pl.BlockSpec((tm, d), lambda j, i: (i, 0)),
                  pl.BlockSpec((d, tn), lambda j, i: (0, j)),
                  pl.BlockSpec((d, tn), lambda j, i: (0, nj + j)),
                  pl.BlockSpec((CONV_WIDTH, tn), lambda j, i: (0, j)),
                  pl.BlockSpec((CONV_WIDTH, tn), lambda j, i: (0, nj + j)),
                  pl.BlockSpec((1, tn), lambda j, i: (0, j)),
                  pl.BlockSpec((1, tn), lambda j, i: (0, nj + j))],
        out_specs=pl.BlockSpec((tm, tn), lambda j, i: (i, j)),
        scratch_shapes=[pltpu.VMEM((SUBLANES, tn), F32),
                        pltpu.VMEM((SUBLANES, tn), F32)],
        compiler_params=_params("parallel", "arbitrary"),
        name="ffn_up",
    )(h2, w_up_bf16, w_up_bf16, conv_w, conv_w, conv_b, conv_b)


def _down_kernel(a_ref, w_ref, x1_ref, g_ref, fw_ref, o_ref):
    y = jnp.dot(a_ref[...], w_ref[...], preferred_element_type=F32)
    x2 = x1_ref[...] + g_ref[...] * y
    ms = jnp.mean(x2 * x2, axis=-1, keepdims=True)
    o_ref[...] = (x2 * lax.rsqrt(ms + EPS)) * fw_ref[...]


def _ffn_down(act, w_down_bf16, x1, mod, final_w, tm=256):
    s, d = x1.shape
    return pl.pallas_call(
        _down_kernel,
        out_shape=jax.ShapeDtypeStruct((s, d), F32),
        grid=(s // tm,),
        in_specs=[pl.BlockSpec((tm, D_FF), lambda i: (i, 0)),
                  pl.BlockSpec((D_FF, d), lambda i: (0, 0), pipeline_mode=pl.Buffered(1)),
                  pl.BlockSpec((tm, d), lambda i: (i, 0)),
                  pl.BlockSpec((1, d), lambda i: (0, 5)),
                  pl.BlockSpec((1, d), lambda i: (0, 0))],
        out_specs=pl.BlockSpec((tm, d), lambda i: (i, 0)),
        compiler_params=_params("parallel"),
        name="ffn_down",
    )(act, w_down_bf16, x1, mod, final_w)


def _rotary_inv_freq():
    inv = 1.0 / (ROPE_THETA ** (jnp.arange(0, ROT_DIM, 2, dtype=F32) / ROT_DIM))
    pad = jnp.zeros((HEAD_DIM - ROT_DIM,), F32)
    return jnp.concatenate([inv, inv, pad]).reshape(1, HEAD_DIM)


def kernel(x, c, positions, w_ada, b_ada, norm1_w, w_in, lb_logits, gnorm_w, w_out,
           norm2_w, w_up, conv_w, conv_b, w_down, final_norm_w):
    b, s, d = x.shape
    assert b == 1 and d == D_MODEL and w_ada.shape[0] == 1
    x2d = x.reshape(s, d)
    pos_col = positions.reshape(s, 1)

    mod = _modulation(c.reshape(d, 1), w_ada[0], b_ada)
    proj = _in_proj(x2d, pos_col, _rotary_inv_freq(), norm1_w, mod, mod, lb_logits,
                    w_in[0].astype(BF16))
    y_attn = _attention(proj)
    y_rec = _recurrence(proj, gnorm_w)
    x1, h2 = _out_proj(x2d, y_attn, y_rec, w_out[0].astype(BF16), mod, norm2_w)
    act = _ffn_up(h2, w_up[0].astype(BF16), conv_w[0], conv_b)
    out = _ffn_down(act, w_down[0].astype(BF16), x1, mod, final_norm_w.reshape(1, d))
    return out.reshape(b, s, d)
```

```python
import functools

import jax
import jax.numpy as jnp
from jax import lax
from jax.experimental import pallas as pl
from jax.experimental.pallas import tpu as pltpu

F32 = jnp.float32
BF16 = jnp.bfloat16

D_MODEL = 2048
N_HEADS = 8
HEAD_DIM = 128
WIDTH = N_HEADS * HEAD_DIM
N_SEG = 7
ROT_DIM = HEAD_DIM // 4
ROPE_THETA = 500000.0
DILATIONS = (1, 4, 16)
WINDOW_STEPS = 128
D_FF = 5632
CONV_WIDTH = 3
N_MOD = 6
EPS = 1e-6
MASKED = -1e30

VMEM_LIMIT = 56 * 1024 * 1024
SUB_N = 256


def _sigmoid(x):
    return 1.0 / (1.0 + jnp.exp(-x))


def _silu(x):
    return x * _sigmoid(x)


def _params(*semantics):
    return pltpu.CompilerParams(dimension_semantics=semantics,
                                vmem_limit_bytes=VMEM_LIMIT)


def _mod_kernel(c_ref, w_ref, b_ref, o_ref):
    ca = _silu(c_ref[...])
    o_ref[...] = jnp.sum(w_ref[...] * ca, axis=0, keepdims=True) + b_ref[...]


def _modulation(c_col, w_ada, b_ada):
    d, n = w_ada.shape
    tn = 512
    return pl.pallas_call(
        _mod_kernel,
        out_shape=jax.ShapeDtypeStruct((1, n), F32),
        grid=(n // tn,),
        in_specs=[pl.BlockSpec((d, 1), lambda j: (0, 0)),
                  pl.BlockSpec((d, tn), lambda j: (0, j)),
                  pl.BlockSpec((1, tn), lambda j: (0, j))],
        out_specs=pl.BlockSpec((1, tn), lambda j: (0, j)),
        compiler_params=_params("parallel"),
        name="mod",
    )(c_col, w_ada, b_ada)


def _norm_modulate(x, norm_w, scale, shift):
    ms = jnp.mean(x * x, axis=-1, keepdims=True)
    return (x * lax.rsqrt(ms + EPS)) * norm_w * (1.0 + scale) + shift


def _in_kernel(x_ref, pos_ref, invf_ref, nw_ref, sh_ref, sc_ref, lbl_ref, w_ref,
               o_ref, h_ref, cos_ref, sina_ref, sinb_ref):
    j = pl.program_id(1)

    @pl.when(j == 0)
    def _():
        h = _norm_modulate(x_ref[...], nw_ref[...], sc_ref[...], sh_ref[...])
        h_ref[...] = h.astype(BF16)
        ang = pos_ref[...].astype(F32) * invf_ref[...]
        lane = lax.broadcasted_iota(jnp.int32, ang.shape, 1)
        c = jnp.cos(ang)
        s = jnp.sin(ang)
        half = ROT_DIM // 2
        cos_ref[...] = jnp.where(lane < ROT_DIM, c, 1.0)
        sina_ref[...] = jnp.where(lane < half, -s, 0.0)
        sinb_ref[...] = jnp.where((lane >= half) & (lane < ROT_DIM), s, 0.0)

    def for_subtiles(epilogue):
        for c0 in range(0, WIDTH, SUB_N):
            cols = slice(c0, c0 + SUB_N)
            r = jnp.dot(h_ref[...], w_ref[:, cols], preferred_element_type=F32)
            o_ref[:, cols] = epilogue(r, cols)

    def rotary(scale):
        half = ROT_DIM // 2

        def epilogue(r, cols):
            outs = []
            for hh in range(SUB_N // HEAD_DIM):
                rh = r[:, hh * HEAD_DIM:(hh + 1) * HEAD_DIM]
                out = (rh * cos_ref[...]
                       + pltpu.roll(rh, HEAD_DIM - half, 1) * sina_ref[...]
                       + pltpu.roll(rh, half, 1) * sinb_ref[...])
                outs.append(out * scale if scale != 1.0 else out)
            return jnp.concatenate(outs, axis=1)
        return epilogue

    @pl.when(j == 0)
    def _():
        for_subtiles(rotary(HEAD_DIM ** -0.5))

    @pl.when(j == 1)
    def _():
        for_subtiles(rotary(1.0))

    @pl.when((j == 2) | (j == 5))
    def _():
        for_subtiles(lambda r, cols: r)

    @pl.when(j == 3)
    def _():
        for_subtiles(lambda r, cols: _silu(r) * (HEAD_DIM ** -0.5))

    @pl.when(j == 4)
    def _():
        lg = lbl_ref[...]
        e = jnp.exp(lg - jnp.max(lg, axis=0, keepdims=True))
        lb = e[0:1, :] / jnp.sum(e, axis=0, keepdims=True)

        def epilogue(r, cols):
            f = lb[:, cols] + (1.0 - lb[:, cols]) * _sigmoid(r)
            return jnp.log(f)
        for_subtiles(epilogue)

    @pl.when(j == 6)
    def _():
        for_subtiles(lambda r, cols: _silu(r))


def _in_proj(x2d, pos_col, invf, norm_w, shift, scale, lb_logits, w_in_bf16, tm=512):
    s, d = x2d.shape
    n = w_in_bf16.shape[1]
    return pl.pallas_call(
        _in_kernel,
        out_shape=jax.ShapeDtypeStruct((s, n), F32),
        grid=(s // tm, n // WIDTH),
        in_specs=[pl.BlockSpec((tm, d), lambda i, j: (i, 0)),
                  pl.BlockSpec((tm, 1), lambda i, j: (i, 0)),
                  pl.BlockSpec((1, HEAD_DIM), lambda i, j: (0, 0)),
                  pl.BlockSpec((1, d), lambda i, j: (0, 0)),
                  pl.BlockSpec((1, d), lambda i, j: (0, 0)),
                  pl.BlockSpec((1, d), lambda i, j: (0, 1)),
                  pl.BlockSpec(lb_logits.shape, lambda i, j: (0, 0)),
                  pl.BlockSpec((d, WIDTH), lambda i, j: (0, j))],
        out_specs=pl.BlockSpec((tm, WIDTH), lambda i, j: (i, j)),
        scratch_shapes=[pltpu.VMEM((tm, d), BF16),
                        pltpu.VMEM((tm, HEAD_DIM), F32),
                        pltpu.VMEM((tm, HEAD_DIM), F32),
                        pltpu.VMEM((tm, HEAD_DIM), F32)],
        compiler_params=_params("parallel", "arbitrary"),
        name="in_proj",
    )(x2d, pos_col, invf, norm_w, shift, scale, lb_logits, w_in_bf16)


ATT_TILE = 2048
ATT_BLK = 128
ATT_ITERS = ATT_TILE // ATT_BLK


def _attn_kernel(q_ref, kp_ref, kc_ref, vp_ref, vc_ref, o_ref, acc_ref, m_ref, l_ref):
    qi = lax.broadcasted_iota(jnp.int32, (ATT_BLK, 2 * ATT_BLK), 0)
    ki = lax.broadcasted_iota(jnp.int32, (ATT_BLK, 2 * ATT_BLK), 1)
    band = (ki >= qi) & (ki <= qi + WINDOW_STEPS)
    band_first = band & ((ki >= ATT_BLK) | (pl.program_id(1) > 0))

    def rows(start, n, d):
        return pl.ds(start, n) if d == 1 else pl.ds(start, n, stride=d)

    for p, d in enumerate(DILATIONS):
        span = ATT_BLK * d
        for r in range(d):
            for blk in range(ATT_TILE // span):
                start = r + blk * span
                qb = q_ref[rows(start, ATT_BLK, d), :].astype(BF16)
                if blk > 0:
                    kk = kc_ref[rows(start - span, 2 * ATT_BLK, d), :].astype(BF16)
                    vv = vc_ref[rows(start - span, 2 * ATT_BLK, d), :].astype(BF16)
                    valid = band
                else:
                    prev = rows(ATT_TILE + start - span, ATT_BLK, d)
                    cur = rows(start, ATT_BLK, d)
                    kk = jnp.concatenate([kp_ref[prev, :].astype(BF16),
                                          kc_ref[cur, :].astype(BF16)], axis=0)
                    vv = jnp.concatenate([vp_ref[prev, :].astype(BF16),
                                          vc_ref[cur, :].astype(BF16)], axis=0)
                    valid = band_first
                s = lax.dot_general(qb, kk, (((1,), (1,)), ((), ())),
                                    preferred_element_type=F32)
                s = jnp.where(valid, s, MASKED)
                m = jnp.max(s, axis=1, keepdims=True)
                pe = jnp.exp(s - m)
                l = jnp.sum(pe, axis=1, keepdims=True)
                acc = jnp.dot(pe.astype(BF16), vv, preferred_element_type=F32)
                out_rows = rows(start, ATT_BLK, d)
                acc_ref[p, out_rows, :] = acc
                m_ref[p, out_rows, :] = jnp.broadcast_to(m, (ATT_BLK, HEAD_DIM))
                l_ref[p, out_rows, :] = jnp.broadcast_to(l, (ATT_BLK, HEAD_DIM))

    m_all = jnp.maximum(jnp.maximum(m_ref[0], m_ref[1]), m_ref[2])
    num = jnp.zeros((ATT_TILE, HEAD_DIM), F32)
    den = jnp.zeros((ATT_TILE, HEAD_DIM), F32)
    for p in range(len(DILATIONS)):
        w = jnp.exp(m_ref[p] - m_all)
        num = num + w * acc_ref[p]
        den = den + w * l_ref[p]
    o_ref[...] = (num / den).astype(o_ref.dtype)


def _attention(proj):
    s = proj.shape[0]
    nt = s // ATT_TILE
    blk = (ATT_TILE, HEAD_DIM)
    prev = lambda t: jnp.maximum(t - 1, 0)
    return pl.pallas_call(
        _attn_kernel,
        out_shape=jax.ShapeDtypeStruct((s, WIDTH), BF16),
        grid=(N_HEADS, nt),
        in_specs=[pl.BlockSpec(blk, lambda h, t: (t, h)),
                  pl.BlockSpec(blk, lambda h, t: (prev(t), N_HEADS + h)),
                  pl.BlockSpec(blk, lambda h, t: (t, N_HEADS + h)),
                  pl.BlockSpec(blk, lambda h, t: (prev(t), 2 * N_HEADS + h)),
                  pl.BlockSpec(blk, lambda h, t: (t, 2 * N_HEADS + h))],
        out_specs=pl.BlockSpec(blk, lambda h, t: (t, h)),
        scratch_shapes=[pltpu.VMEM((len(DILATIONS), ATT_TILE, HEAD_DIM), F32),
                        pltpu.VMEM((len(DILATIONS), ATT_TILE, HEAD_DIM), F32),
                        pltpu.VMEM((len(DILATIONS), ATT_TILE, HEAD_DIM), F32)],
        compiler_params=_params("parallel", "parallel"),
        name="attn",
    )(proj, proj, proj, proj, proj)


REC_CHUNK = 128
REC_LEVELS = 7
REC_ROWS = 1024
REC_HEADS = 4
SUBLANES = 8


def _block_scans(lf):
    c = lf.shape[0]
    row = lax.broadcasted_iota(jnp.int32, (c, HEAD_DIM), 0)
    fwd, rev = [lf], [lf]
    x, y = lf, lf
    for j in range(REC_LEVELS):
        m = 1 << j
        if m < SUBLANES:
            x3 = x.reshape(c // SUBLANES, SUBLANES, HEAD_DIM)
            y3 = y.reshape(c // SUBLANES, SUBLANES, HEAD_DIM)
            addx = jnp.zeros_like(x)
            addy = jnp.zeros_like(y)
            for k in range(m):
                sx = pltpu.roll(x3, k + 1, 1).reshape(c, HEAD_DIM)
                addx = jnp.where((row & (2 * m - 1)) == m + k, sx, addx)
                sy = pltpu.roll(y3, SUBLANES - (k + 1), 1).reshape(c, HEAD_DIM)
                addy = jnp.where((row & (2 * m - 1)) == m - 1 - k, sy, addy)
            x = x + addx
            y = y + addy
        else:
            xs, ys = [], []
            for b in range(c // (2 * m)):
                lo = x[b * 2 * m: b * 2 * m + m]
                hi = x[b * 2 * m + m: (b + 1) * 2 * m]
                xs += [lo, hi + lo[m - 1:m, :]]
                lo = y[b * 2 * m: b * 2 * m + m]
                hi = y[b * 2 * m + m: (b + 1) * 2 * m]
                ys += [lo + hi[0:1, :], hi]
            x = jnp.concatenate(xs, axis=0)
            y = jnp.concatenate(ys, axis=0)
        fwd.append(x)
        rev.append(y)
    return fwd, rev


def _rec_kernel(q_ref, lf_ref, v_ref, g_ref, gw_ref, o_ref, state_ref):
    @pl.when(pl.program_id(1) == 0)
    def _():
        state_ref[...] = jnp.zeros_like(state_ref)

    c = REC_CHUNK
    ti = lax.broadcasted_iota(jnp.int32, (c, c), 0)
    si = lax.broadcasted_iota(jnp.int32, (c, c), 1)
    diff = ti ^ si
    level = jnp.full((c, c), -1, jnp.int32)
    for j in range(REC_LEVELS):
        level = jnp.where((diff >> j) == 1, j, level)
    level = jnp.where(si > ti, -1, jnp.where(si == ti, REC_LEVELS, level))
    nt_dims = (((1,), (1,)), ((), ()))

    def one_head(hh, rows):
        lanes = slice(hh * HEAD_DIM, (hh + 1) * HEAD_DIM)
        qf = q_ref[rows, lanes]
        lf = lf_ref[rows, lanes]
        v = v_ref[rows, lanes].astype(BF16)
        kk = 1.0 - jnp.exp(lf)
        fwd, rev = _block_scans(lf)
        b = fwd[REC_LEVELS]

        a = jnp.where(level == REC_LEVELS,
                      lax.dot_general(qf.astype(BF16), kk.astype(BF16), nt_dims,
                                      preferred_element_type=F32), 0.0)
        for j in range(REC_LEVELS):
            ql = (qf * jnp.exp(fwd[j])).astype(BF16)
            kl = (kk * jnp.exp(rev[j] - lf)).astype(BF16)
            al = lax.dot_general(ql, kl, nt_dims, preferred_element_type=F32)
            a = jnp.where(level == j, al, a)

        state_t = state_ref[hh]
        o = (lax.dot_general((qf * jnp.exp(b)).astype(BF16), state_t.astype(BF16),
                             nt_dims, preferred_element_type=F32)
             + jnp.dot(a.astype(BF16), v, preferred_element_type=F32))
        b_last = b[c - 1:c, :]
        kd = (kk * jnp.exp(b_last - b)).astype(BF16)
        upd_t = lax.dot_general(v, kd, (((0,), (0,)), ((), ())),
                                preferred_element_type=F32)
        state_ref[hh] = jnp.exp(b_last) * state_t + upd_t

        o = o * lax.rsqrt(jnp.mean(o * o, axis=-1, keepdims=True) + EPS) * gw_ref[...]
        o_ref[rows, lanes] = (o * g_ref[rows, lanes]).astype(o_ref.dtype)

    def chunk(ci, carry):
        rows = pl.ds(pl.multiple_of(ci * c, c), c)
        for hh in range(REC_HEADS):
            one_head(hh, rows)
        return carry

    lax.fori_loop(0, REC_ROWS // c, chunk, 0)


def _recurrence(proj, gnorm_w):
    s = proj.shape[0]
    blk = (REC_ROWS, REC_HEADS * HEAD_DIM)
    groups = N_HEADS // REC_HEADS
    seg = lambda k: (lambda h, i: (i, k * groups + h))
    return pl.pallas_call(
        _rec_kernel,
        out_shape=jax.ShapeDtypeStruct((s, WIDTH), BF16),
        grid=(groups, s // REC_ROWS),
        in_specs=[pl.BlockSpec(blk, seg(3)),
                  pl.BlockSpec(blk, seg(4)),
                  pl.BlockSpec(blk, seg(5)),
                  pl.BlockSpec(blk, seg(6)),
                  pl.BlockSpec((1, HEAD_DIM), lambda h, i: (0, 0))],
        out_specs=pl.BlockSpec(blk, lambda h, i: (i, h)),
        scratch_shapes=[pltpu.VMEM((REC_HEADS, HEAD_DIM, HEAD_DIM), F32)],
        compiler_params=_params("parallel", "arbitrary"),
        name="hgrn",
    )(proj, proj, proj, proj, gnorm_w)


def _out_kernel(x_ref, ya_ref, yr_ref, w_ref, g_ref, nw_ref, sh_ref, sc_ref,
                x1_ref, h2_ref):
    y = (jnp.dot(ya_ref[...], w_ref[0:WIDTH, :], preferred_element_type=F32)
         + jnp.dot(yr_ref[...], w_ref[WIDTH:, :], preferred_element_type=F32))
    x1 = x_ref[...] + g_ref[...] * y
    x1_ref[...] = x1
    h2_ref[...] = _norm_modulate(x1, nw_ref[...], sc_ref[...], sh_ref[...]).astype(BF16)


def _out_proj(x2d, y_attn, y_rec, w_out_bf16, mod, norm2_w, tm=256):
    s, d = x2d.shape
    vec = lambda k: pl.BlockSpec((1, d), lambda i: (0, k))
    return pl.pallas_call(
        _out_kernel,
        out_shape=(jax.ShapeDtypeStruct((s, d), F32),
                   jax.ShapeDtypeStruct((s, d), BF16)),
        grid=(s // tm,),
        in_specs=[pl.BlockSpec((tm, d), lambda i: (i, 0)),
                  pl.BlockSpec((tm, WIDTH), lambda i: (i, 0)),
                  pl.BlockSpec((tm, WIDTH), lambda i: (i, 0)),
                  pl.BlockSpec((2 * WIDTH, d), lambda i: (0, 0)),
                  vec(2),
                  pl.BlockSpec((1, d), lambda i: (0, 0)),
                  vec(3), vec(4)],
        out_specs=(pl.BlockSpec((tm, d), lambda i: (i, 0)),
                   pl.BlockSpec((tm, d), lambda i: (i, 0))),
        compiler_params=_params("parallel"),
        name="out_proj",
    )(x2d, y_attn, y_rec, w_out_bf16, mod, norm2_w, mod, mod)


LANES = 128


def _conv_rows(stage, half, cols, cw_ref, cb_ref):
    w0, w1, w2 = cw_ref[0:1, cols], cw_ref[1:2, cols], cw_ref[2:3, cols]
    b = cb_ref[:, cols]
    r = [stage[pl.ds(SUBLANES - 2 + k, half, stride=2), :] for k in range(4)]
    even = r[2] * w2 + r[1] * w1 + r[0] * w0 + b
    odd = r[3] * w2 + r[2] * w1 + r[1] * w0 + b
    return even, odd


def _up_kernel(h_ref, wg_ref, wu_ref, cwg_ref, cwu_ref, cbg_ref, cbu_ref, o_ref,
               tailg_ref, tailu_ref, stage_ref, act_ref):
    @pl.when(pl.program_id(1) == 0)
    def _():
        tailg_ref[...] = jnp.zeros_like(tailg_ref)
        tailu_ref[...] = jnp.zeros_like(tailu_ref)

    tm, n = o_ref.shape
    half = tm // 2
    for c in range(n // SUB_N):
        cols = slice(c * SUB_N, (c + 1) * SUB_N)
        g = jnp.dot(h_ref[...], wg_ref[:, cols], preferred_element_type=F32)
        u = jnp.dot(h_ref[...], wu_ref[:, cols], preferred_element_type=F32)
        for lh in range(SUB_N // LANES):
            lanes = slice(c * SUB_N + lh * LANES, c * SUB_N + (lh + 1) * LANES)
            sg = stage_ref.at[c % 2, 0, lh]
            su = stage_ref.at[c % 2, 1, lh]
            sg[0:SUBLANES, :] = tailg_ref[:, lanes]
            su[0:SUBLANES, :] = tailu_ref[:, lanes]
            sg[SUBLANES:, :] = g[:, lh * LANES:(lh + 1) * LANES]
            su[SUBLANES:, :] = u[:, lh * LANES:(lh + 1) * LANES]
            tailg_ref[:, lanes] = sg[tm:, :]
            tailu_ref[:, lanes] = su[tm:, :]
            plane = act_ref.at[lh]
            gates = _conv_rows(sg, half, lanes, cwg_ref, cbg_ref)
            ups = _conv_rows(su, half, lanes, cwu_ref, cbu_ref)
            for parity in range(2):
                plane[pl.ds(parity, half, stride=2), :] = _silu(gates[parity]) * ups[parity]
            o_ref[:, lanes] = plane[...].astype(o_ref.dtype)


def _ffn_up(h2, w_up_bf16, conv_w, conv_b, tm=256, tn=D_FF // 2):
    s, d = h2.shape
    nj = D_FF // tn
    once = pl.Buffered(1)
    return pl.pallas_call(
        _up_kernel,
        out_shape=jax.ShapeDtypeStruct((s, D_FF), BF16),
        grid=(nj, s // tm),
        in_specs=[pl.BlockSpec((tm, d), lambda j, i: (i, 0)),
                  pl.BlockSpec((d, tn), lambda j, i: (0, j), pipeline_mode=once),
                  pl.BlockSpec((d, tn), lambda j, i: (0, nj + j), pipeline_mode=once),
                  pl.BlockSpec((CONV_WIDTH, tn), lambda j, i: (0, j)),
                  pl.BlockSpec((CONV_WIDTH, tn), lambda j, i: (0, nj + j)),
                  pl.BlockSpec((1, tn), lambda j, i: (0, j)),
                  pl.BlockSpec((1, tn), lambda j, i: (0, nj + j))],
        out_specs=pl.BlockSpec((tm, tn), lambda j, i: (i, j)),
        scratch_shapes=[pltpu.VMEM((SUBLANES, tn), F32),
                        pltpu.VMEM((SUBLANES, tn), F32),
                        pltpu.VMEM((2, 2, SUB_N // LANES, SUBLANES + tm, LANES), F32),
                        pltpu.VMEM((SUB_N // LANES, tm, LANES), F32)],
        compiler_params=_params("parallel", "arbitrary"),
        name="ffn_up",
    )(h2, w_up_bf16, w_up_bf16, conv_w, conv_w, conv_b, conv_b)


def _down_kernel(a_ref, w_ref, x1_ref, g_ref, fw_ref, o_ref):
    y = jnp.dot(a_ref[...], w_ref[...], preferred_element_type=F32)
    x2 = x1_ref[...] + g_ref[...] * y
    ms = jnp.mean(x2 * x2, axis=-1, keepdims=True)
    o_ref[...] = (x2 * lax.rsqrt(ms + EPS)) * fw_ref[...]


def _ffn_down(act, w_down_bf16, x1, mod, final_w, tm=256):
    s, d = x1.shape
    return pl.pallas_call(
        _down_kernel,
        out_shape=jax.ShapeDtypeStruct((s, d), F32),
        grid=(s // tm,),
        in_specs=[pl.BlockSpec((tm, D_FF), lambda i: (i, 0)),
                  pl.BlockSpec((D_FF, d), lambda i: (0, 0), pipeline_mode=pl.Buffered(1)),
                  pl.BlockSpec((tm, d), lambda i: (i, 0)),
                  pl.BlockSpec((1, d), lambda i: (0, 5)),
                  pl.BlockSpec((1, d), lambda i: (0, 0))],
        out_specs=pl.BlockSpec((tm, d), lambda i: (i, 0)),
        compiler_params=_params("parallel"),
        name="ffn_down",
    )(act, w_down_bf16, x1, mod, final_w)


def _rotary_inv_freq():
    inv = 1.0 / (ROPE_THETA ** (jnp.arange(0, ROT_DIM, 2, dtype=F32) / ROT_DIM))
    pad = jnp.zeros((HEAD_DIM - ROT_DIM,), F32)
    return jnp.concatenate([inv, inv, pad]).reshape(1, HEAD_DIM)


def kernel(x, c, positions, w_ada, b_ada, norm1_w, w_in, lb_logits, gnorm_w, w_out,
           norm2_w, w_up, conv_w, conv_b, w_down, final_norm_w):
    b, s, d = x.shape
    assert b == 1 and d == D_MODEL and w_ada.shape[0] == 1
    x2d = x.reshape(s, d)
    pos_col = positions.reshape(s, 1)

    mod = _modulation(c.reshape(d, 1), w_ada[0], b_ada)
    proj = _in_proj(x2d, pos_col, _rotary_inv_freq(), norm1_w, mod, mod, lb_logits,
                    w_in[0].astype(BF16))
    y_attn = _attention(proj)
    y_rec = _recurrence(proj, gnorm_w)
    x1, h2 = _out_proj(x2d, y_attn, y_rec, w_out[0].astype(BF16), mod, norm2_w)
    act = _ffn_up(h2, w_up[0].astype(BF16), conv_w[0], conv_b)
    out = _ffn_down(act, w_down[0].astype(BF16), x1, mod, final_norm_w.reshape(1, d))
    return out.reshape(b, s, d)
```

```python
import jax
import jax.numpy as jnp
from jax import lax
from jax.experimental import pallas as pl
from jax.experimental.pallas import tpu as pltpu

F32 = jnp.float32
BF16 = jnp.bfloat16

D_MODEL = 2048
N_HEADS = 8
HEAD_DIM = 128
WIDTH = N_HEADS * HEAD_DIM
ROT_DIM = HEAD_DIM // 4
ROPE_THETA = 500000.0
DILATIONS = (1, 4, 16)
WINDOW_STEPS = 128
D_FF = 5632
CONV_WIDTH = 3
EPS = 1e-6
MASKED = -1e30
LOG2_E = 1.4426950408889634

VMEM_LIMIT = 56 * 1024 * 1024
SUB_N = 256
LANES = 128
SUBLANES = 8


def _sigmoid(x):
    return 1.0 / (1.0 + jnp.exp(-x))


def _silu(x):
    return x * _sigmoid(x)


def _params(*semantics):
    return pltpu.CompilerParams(dimension_semantics=semantics,
                                vmem_limit_bytes=VMEM_LIMIT)


def _mod_kernel(c_ref, w_ref, b_ref, o_ref):
    ca = _silu(c_ref[...])
    o_ref[...] = jnp.sum(w_ref[...] * ca, axis=0, keepdims=True) + b_ref[...]


def _modulation(c_col, w_ada, b_ada):
    d, n = w_ada.shape
    tn = 512
    return pl.pallas_call(
        _mod_kernel,
        out_shape=jax.ShapeDtypeStruct((1, n), F32),
        grid=(n // tn,),
        in_specs=[pl.BlockSpec((d, 1), lambda j: (0, 0)),
                  pl.BlockSpec((d, tn), lambda j: (0, j)),
                  pl.BlockSpec((1, tn), lambda j: (0, j))],
        out_specs=pl.BlockSpec((1, tn), lambda j: (0, j)),
        compiler_params=_params("parallel"),
        name="mod",
    )(c_col, w_ada, b_ada)


def _norm_modulate(x, norm_w, scale, shift):
    ms = jnp.mean(x * x, axis=-1, keepdims=True)
    return (x * lax.rsqrt(ms + EPS)) * norm_w * (1.0 + scale) + shift


SEG_AQ, SEG_AV, SEG_AK, SEG_HI, SEG_HF, SEG_HQ, SEG_HG = range(7)
SEG_SOURCE = (0, 2, 1, 5, 4, 3, 6)


def _in_kernel(x_ref, pos_ref, invf_ref, nw_ref, sh_ref, sc_ref, lbl_ref, w_ref,
               o_ref, h_ref, cos_ref, sina_ref, sinb_ref):
    j = pl.program_id(1)

    @pl.when(j == 0)
    def _():
        h = _norm_modulate(x_ref[...], nw_ref[...], sc_ref[...], sh_ref[...])
        h_ref[...] = h.astype(BF16)
        ang = pos_ref[...].astype(F32) * invf_ref[...]
        lane = lax.broadcasted_iota(jnp.int32, ang.shape, 1)
        c = jnp.cos(ang)
        s = jnp.sin(ang)
        half = ROT_DIM // 2
        cos_ref[...] = jnp.where(lane < ROT_DIM, c, 1.0)
        sina_ref[...] = jnp.where(lane < half, -s, 0.0)
        sinb_ref[...] = jnp.where((lane >= half) & (lane < ROT_DIM), s, 0.0)

    def segment(seg, epilogue):
        half = seg % 2
        for c0 in range(0, WIDTH, SUB_N):
            cols = slice(c0, c0 + SUB_N)
            r = jnp.dot(h_ref[...], w_ref[:, seg * WIDTH + c0:seg * WIDTH + c0 + SUB_N],
                        preferred_element_type=F32)
            o_ref[:, half * WIDTH + c0:half * WIDTH + c0 + SUB_N] = epilogue(r, cols)

    def rotary(scale):
        half = ROT_DIM // 2

        def epilogue(r, cols):
            outs = []
            for hh in range(SUB_N // HEAD_DIM):
                rh = r[:, hh * HEAD_DIM:(hh + 1) * HEAD_DIM]
                out = (rh * cos_ref[...]
                       + pltpu.roll(rh, HEAD_DIM - half, 1) * sina_ref[...]
                       + pltpu.roll(rh, half, 1) * sinb_ref[...])
                outs.append(out * scale if scale != 1.0 else out)
            return jnp.concatenate(outs, axis=1)
        return epilogue

    def plain(r, cols):
        return r

    @pl.when(j == 0)
    def _():
        segment(SEG_AQ, rotary(HEAD_DIM ** -0.5 * LOG2_E))
        segment(SEG_AV, plain)

    @pl.when(j == 1)
    def _():
        segment(SEG_AK, rotary(1.0))
        segment(SEG_HI, plain)

    @pl.when(j == 2)
    def _():
        lg = lbl_ref[...]
        e = jnp.exp(lg - jnp.max(lg, axis=0, keepdims=True))
        lb = e[0:1, :] / jnp.sum(e, axis=0, keepdims=True)

        def log2_forget(r, cols):
            f = lb[:, cols] + (1.0 - lb[:, cols]) * _sigmoid(r)
            return jnp.log2(f)
        segment(SEG_HF, log2_forget)
        segment(SEG_HQ, lambda r, cols: _silu(r) * (HEAD_DIM ** -0.5))

    @pl.when(j == 3)
    def _():
        segment(SEG_HG, lambda r, cols: _silu(r))
        o_ref[:, WIDTH:] = jnp.zeros((o_ref.shape[0], WIDTH), F32)


def _arrange_w_in(w_in):
    segs = [w_in[:, k * WIDTH:(k + 1) * WIDTH] for k in SEG_SOURCE]
    return jnp.concatenate(segs, axis=1).astype(BF16)


def _in_proj(x2d, pos_col, invf, norm_w, shift, scale, lb_logits, w_in_bf16, tm=512):
    s, d = x2d.shape
    n = w_in_bf16.shape[1]
    steps = pl.cdiv(n, 2 * WIDTH)
    return pl.pallas_call(
        _in_kernel,
        out_shape=jax.ShapeDtypeStruct((s, steps * 2 * WIDTH), F32),
        grid=(s // tm, steps),
        in_specs=[pl.BlockSpec((tm, d), lambda i, j: (i, 0)),
                  pl.BlockSpec((tm, 1), lambda i, j: (i, 0)),
                  pl.BlockSpec((1, HEAD_DIM), lambda i, j: (0, 0)),
                  pl.BlockSpec((1, d), lambda i, j: (0, 0)),
                  pl.BlockSpec((1, d), lambda i, j: (0, 0)),
                  pl.BlockSpec((1, d), lambda i, j: (0, 1)),
                  pl.BlockSpec(lb_logits.shape, lambda i, j: (0, 0)),
                  pl.BlockSpec((d, n), lambda i, j: (0, 0), pipeline_mode=pl.Buffered(1))],
        out_specs=pl.BlockSpec((tm, 2 * WIDTH), lambda i, j: (i, j)),
        scratch_shapes=[pltpu.VMEM((tm, d), BF16),
                        pltpu.VMEM((tm, HEAD_DIM), F32),
                        pltpu.VMEM((tm, HEAD_DIM), F32),
                        pltpu.VMEM((tm, HEAD_DIM), F32)],
        compiler_params=_params("parallel", "arbitrary"),
        name="in_proj",
    )(x2d, pos_col, invf, norm_w, shift, scale, lb_logits, w_in_bf16)


ATT_TILE = 2048
ATT_BLK = 128
ATT_P4 = 4


def _attn_kernel(q_ref, k_ref, v_ref, o_ref,
                 q4_ref, k4_ref, v4_ref, k16_ref, v16_ref, ktail_ref, vtail_ref,
                 acc_ref, m_ref, l_ref, y_ref):
    t = pl.program_id(1)
    cur = t % 2
    prev = 1 - cur
    span4 = ATT_TILE // ATT_P4

    @pl.when(t == 0)
    def _():
        k4_ref[1] = jnp.zeros_like(k4_ref[1])
        v4_ref[1] = jnp.zeros_like(v4_ref[1])
        k16_ref[1] = jnp.zeros_like(k16_ref[1])
        v16_ref[1] = jnp.zeros_like(v16_ref[1])
        ktail_ref[...] = jnp.zeros_like(ktail_ref)
        vtail_ref[...] = jnp.zeros_like(vtail_ref)

    qi = lax.broadcasted_iota(jnp.int32, (ATT_BLK, 2 * ATT_BLK), 0)
    ki = lax.broadcasted_iota(jnp.int32, (ATT_BLK, 2 * ATT_BLK), 1)
    band = (ki >= qi) & (ki <= qi + WINDOW_STEPS)
    band_first = band & ((ki >= ATT_BLK) | (t > 0))
    nt_dims = (((1,), (1,)), ((), ()))
    ones = jnp.ones((2 * ATT_BLK, HEAD_DIM), BF16)

    def block(qb, kk, vv, valid):
        s = lax.dot_general(qb, kk, nt_dims, preferred_element_type=F32)
        s = jnp.where(valid, s, MASKED)
        m = jnp.max(s, axis=1, keepdims=True)
        pe = jnp.exp2(s - m).astype(BF16)
        ext = jnp.dot(pe, jnp.concatenate([vv, ones], axis=1), preferred_element_type=F32)
        return (ext[:, :HEAD_DIM], jnp.broadcast_to(m, (ATT_BLK, HEAD_DIM)),
                ext[:, HEAD_DIM:])

    def put(p, rows, res):
        acc_ref[p, rows, :], m_ref[p, rows, :], l_ref[p, rows, :] = res

    for r in range(ATT_P4):
        src = pl.ds(r, span4, stride=ATT_P4)
        dst = pl.ds(r * span4, span4)
        q4_ref[dst, :] = q_ref[src, :]
        k4_ref[cur, dst, :] = k_ref[src, :]
        v4_ref[cur, dst, :] = v_ref[src, :]

    for b in range(ATT_TILE // ATT_BLK):
        rows = pl.ds(b * ATT_BLK, ATT_BLK)
        qb = q_ref[rows, :].astype(BF16)
        if b > 0:
            both = pl.ds((b - 1) * ATT_BLK, 2 * ATT_BLK)
            kk = k_ref[both, :].astype(BF16)
            vv = v_ref[both, :].astype(BF16)
            valid = band
        else:
            kk = jnp.concatenate([ktail_ref[...], k_ref[rows, :].astype(BF16)], axis=0)
            vv = jnp.concatenate([vtail_ref[...], v_ref[rows, :].astype(BF16)], axis=0)
            valid = band_first
        put(0, rows, block(qb, kk, vv, valid))
    last = pl.ds(ATT_TILE - ATT_BLK, ATT_BLK)
    ktail_ref[...] = k_ref[last, :].astype(BF16)
    vtail_ref[...] = v_ref[last, :].astype(BF16)

    for r in range(ATT_P4):
        for b in range(span4 // ATT_BLK):
            base = r * span4 + b * ATT_BLK
            rows = pl.ds(base, ATT_BLK)
            qb = q4_ref[rows, :].astype(BF16)
            if b > 0:
                both = pl.ds(base - ATT_BLK, 2 * ATT_BLK)
                kk = k4_ref[cur, both, :].astype(BF16)
                vv = v4_ref[cur, both, :].astype(BF16)
                valid = band
            else:
                back = pl.ds((r + 1) * span4 - ATT_BLK, ATT_BLK)
                kk = jnp.concatenate([k4_ref[prev, back, :].astype(BF16),
                                      k4_ref[cur, rows, :].astype(BF16)], axis=0)
                vv = jnp.concatenate([v4_ref[prev, back, :].astype(BF16),
                                      v4_ref[cur, rows, :].astype(BF16)], axis=0)
                valid = band_first
            put(1, rows, block(qb, kk, vv, valid))

    for r in range(ATT_P4):
        for r2 in range(ATT_P4):
            src = pl.ds(r * span4 + r2, ATT_BLK, stride=ATT_P4)
            dst = pl.ds((ATT_P4 * r2 + r) * ATT_BLK, ATT_BLK)
            qb = q4_ref[src, :].astype(BF16)
            kc = k4_ref[cur, src, :].astype(BF16)
            vc = v4_ref[cur, src, :].astype(BF16)
            k16_ref[cur, dst, :] = kc
            v16_ref[cur, dst, :] = vc
            kk = jnp.concatenate([k16_ref[prev, dst, :], kc], axis=0)
            vv = jnp.concatenate([v16_ref[prev, dst, :], vc], axis=0)
            put(2, src, block(qb, kk, vv, band_first))

    for r in range(ATT_P4):
        for b in range(span4 // ATT_BLK):
            rows4 = pl.ds(r * span4 + b * ATT_BLK, ATT_BLK)
            nat = pl.ds(r + ATT_P4 * b * ATT_BLK, ATT_BLK, stride=ATT_P4)
            ms = [m_ref[0, nat, :], m_ref[1, rows4, :], m_ref[2, rows4, :]]
            ls = [l_ref[0, nat, :], l_ref[1, rows4, :], l_ref[2, rows4, :]]
            accs = [acc_ref[0, nat, :], acc_ref[1, rows4, :], acc_ref[2, rows4, :]]
            m_all = jnp.maximum(jnp.maximum(ms[0], ms[1]), ms[2])
            num = jnp.zeros((ATT_BLK, HEAD_DIM), F32)
            den = jnp.zeros((ATT_BLK, HEAD_DIM), F32)
            for p in range(len(DILATIONS)):
                w = jnp.exp2(ms[p] - m_all)
                num = num + w * accs[p]
                den = den + w * ls[p]
            y_ref[nat, :] = num / den
    o_ref[...] = y_ref[...].astype(o_ref.dtype)


def _attention(proj):
    s = proj.shape[0]
    nt = s // ATT_TILE
    blk = (ATT_TILE, HEAD_DIM)
    n_pat = len(DILATIONS)
    return pl.pallas_call(
        _attn_kernel,
        out_shape=jax.ShapeDtypeStruct((s, WIDTH), BF16),
        grid=(N_HEADS, nt),
        in_specs=[pl.BlockSpec(blk, lambda h, t: (t, SEG_AQ * N_HEADS + h)),
                  pl.BlockSpec(blk, lambda h, t: (t, SEG_AK * N_HEADS + h)),
                  pl.BlockSpec(blk, lambda h, t: (t, SEG_AV * N_HEADS + h))],
        out_specs=pl.BlockSpec(blk, lambda h, t: (t, h)),
        scratch_shapes=[pltpu.VMEM(blk, F32),
                        pltpu.VMEM((2,) + blk, F32),
                        pltpu.VMEM((2,) + blk, F32),
                        pltpu.VMEM((2,) + blk, BF16),
                        pltpu.VMEM((2,) + blk, BF16),
                        pltpu.VMEM((ATT_BLK, HEAD_DIM), BF16),
                        pltpu.VMEM((ATT_BLK, HEAD_DIM), BF16),
                        pltpu.VMEM((n_pat,) + blk, F32),
                        pltpu.VMEM((n_pat,) + blk, F32),
                        pltpu.VMEM((n_pat,) + blk, F32),
                        pltpu.VMEM(blk, F32)],
        compiler_params=_params("parallel", "arbitrary"),
        name="attn",
    )(proj, proj, proj)


REC_CHUNK = 128
REC_LEVELS = 7
REC_ROWS = 1024
REC_HEADS = 4


def _block_scans(lf):
    c = lf.shape[0]
    row = lax.broadcasted_iota(jnp.int32, (c, HEAD_DIM), 0)
    fwd, rev = [lf], [lf]
    x, y = lf, lf
    for j in range(REC_LEVELS):
        m = 1 << j
        if m < SUBLANES:
            x3 = x.reshape(c // SUBLANES, SUBLANES, HEAD_DIM)
            y3 = y.reshape(c // SUBLANES, SUBLANES, HEAD_DIM)
            addx = jnp.zeros_like(x)
            addy = jnp.zeros_like(y)
            for k in range(m):
                sx = pltpu.roll(x3, k + 1, 1).reshape(c, HEAD_DIM)
                addx = jnp.where((row & (2 * m - 1)) == m + k, sx, addx)
                sy = pltpu.roll(y3, SUBLANES - (k + 1), 1).reshape(c, HEAD_DIM)
                addy = jnp.where((row & (2 * m - 1)) == m - 1 - k, sy, addy)
            x = x + addx
            y = y + addy
        else:
            xs, ys = [], []
            for b in range(c // (2 * m)):
                lo = x[b * 2 * m: b * 2 * m + m]
                hi = x[b * 2 * m + m: (b + 1) * 2 * m]
                xs += [lo, hi + lo[m - 1:m, :]]
                lo = y[b * 2 * m: b * 2 * m + m]
                hi = y[b * 2 * m + m: (b + 1) * 2 * m]
                ys += [lo + hi[0:1, :], hi]
            x = jnp.concatenate(xs, axis=0)
            y = jnp.concatenate(ys, axis=0)
        fwd.append(x)
        rev.append(y)
    return fwd, rev


def _rec_kernel(q_ref, lf_ref, v_ref, g_ref, gw_ref, o_ref, state_ref):
    @pl.when(pl.program_id(1) == 0)
    def _():
        state_ref[...] = jnp.zeros_like(state_ref)

    c = REC_CHUNK
    ti = lax.broadcasted_iota(jnp.int32, (c, c), 0)
    si = lax.broadcasted_iota(jnp.int32, (c, c), 1)
    diff = ti ^ si
    level = jnp.full((c, c), -1, jnp.int32)
    for j in range(REC_LEVELS):
        level = jnp.where((diff >> j) == 1, j, level)
    level = jnp.where(si > ti, -1, jnp.where(si == ti, REC_LEVELS, level))
    nt_dims = (((1,), (1,)), ((), ()))

    def one_head(hh, rows):
        lanes = slice(hh * HEAD_DIM, (hh + 1) * HEAD_DIM)
        qf = q_ref[rows, lanes]
        lf = lf_ref[rows, lanes]
        v = v_ref[rows, lanes].astype(BF16)
        kk = 1.0 - jnp.exp2(lf)
        fwd, rev = _block_scans(lf)
        b = fwd[REC_LEVELS]

        a = jnp.where(level == REC_LEVELS,
                      lax.dot_general(qf.astype(BF16), kk.astype(BF16), nt_dims,
                                      preferred_element_type=F32), 0.0)
        for j in range(REC_LEVELS):
            ql = (qf * jnp.exp2(fwd[j])).astype(BF16)
            kl = (kk * jnp.exp2(rev[j] - lf)).astype(BF16)
            al = lax.dot_general(ql, kl, nt_dims, preferred_element_type=F32)
            a = jnp.where(level == j, al, a)

        state_t = state_ref[hh]
        o = (lax.dot_general((qf * jnp.exp2(b)).astype(BF16), state_t.astype(BF16),
                             nt_dims, preferred_element_type=F32)
             + jnp.dot(a.astype(BF16), v, preferred_element_type=F32))
        b_last = b[c - 1:c, :]
        kd = (kk * jnp.exp2(b_last - b)).astype(BF16)
        upd_t = lax.dot_general(v, kd, (((0,), (0,)), ((), ())),
                                preferred_element_type=F32)
        state_ref[hh] = jnp.exp2(b_last) * state_t + upd_t

        o = o * lax.rsqrt(jnp.mean(o * o, axis=-1, keepdims=True) + EPS) * gw_ref[...]
        o_ref[rows, lanes] = (o * g_ref[rows, lanes]).astype(o_ref.dtype)

    def chunk(ci, carry):
        rows = pl.ds(pl.multiple_of(ci * c, c), c)
        for hh in range(REC_HEADS):
            one_head(hh, rows)
        return carry

    lax.fori_loop(0, REC_ROWS // c, chunk, 0)


def _recurrence(proj, gnorm_w):
    s = proj.shape[0]
    blk = (REC_ROWS, REC_HEADS * HEAD_DIM)
    groups = N_HEADS // REC_HEADS
    seg = lambda k: (lambda h, i: (i, k * groups + h))
    return pl.pallas_call(
        _rec_kernel,
        out_shape=jax.ShapeDtypeStruct((s, WIDTH), BF16),
        grid=(groups, s // REC_ROWS),
        in_specs=[pl.BlockSpec(blk, seg(SEG_HQ)),
                  pl.BlockSpec(blk, seg(SEG_HF)),
                  pl.BlockSpec(blk, seg(SEG_HI)),
                  pl.BlockSpec(blk, seg(SEG_HG)),
                  pl.BlockSpec((1, HEAD_DIM), lambda h, i: (0, 0))],
        out_specs=pl.BlockSpec(blk, lambda h, i: (i, h)),
        scratch_shapes=[pltpu.VMEM((REC_HEADS, HEAD_DIM, HEAD_DIM), F32)],
        compiler_params=_params("parallel", "arbitrary"),
        name="hgrn",
    )(proj, proj, proj, proj, gnorm_w)


def _out_kernel(x_ref, ya_ref, yr_ref, w_ref, g_ref, nw_ref, sh_ref, sc_ref,
                x1_ref, h2_ref):
    y = (jnp.dot(ya_ref[...], w_ref[0:WIDTH, :], preferred_element_type=F32)
         + jnp.dot(yr_ref[...], w_ref[WIDTH:, :], preferred_element_type=F32))
    x1 = x_ref[...] + g_ref[...] * y
    x1_ref[...] = x1
    h2_ref[...] = _norm_modulate(x1, nw_ref[...], sc_ref[...], sh_ref[...]).astype(BF16)


def _out_proj(x2d, y_attn, y_rec, w_out_bf16, mod, norm2_w, tm=256):
    s, d = x2d.shape
    vec = lambda k: pl.BlockSpec((1, d), lambda i: (0, k))
    return pl.pallas_call(
        _out_kernel,
        out_shape=(jax.ShapeDtypeStruct((s, d), F32),
                   jax.ShapeDtypeStruct((s, d), BF16)),
        grid=(s // tm,),
        in_specs=[pl.BlockSpec((tm, d), lambda i: (i, 0)),
                  pl.BlockSpec((tm, WIDTH), lambda i: (i, 0)),
                  pl.BlockSpec((tm, WIDTH), lambda i: (i, 0)),
                  pl.BlockSpec((2 * WIDTH, d), lambda i: (0, 0)),
                  vec(2),
                  pl.BlockSpec((1, d), lambda i: (0, 0)),
                  vec(3), vec(4)],
        out_specs=(pl.BlockSpec((tm, d), lambda i: (i, 0)),
                   pl.BlockSpec((tm, d), lambda i: (i, 0))),
        compiler_params=_params("parallel"),
        name="out_proj",
    )(x2d, y_attn, y_rec, w_out_bf16, mod, norm2_w, mod, mod)


def _conv_rows(stage, half, cols, cw_ref, cb_ref):
    w0, w1, w2 = cw_ref[0:1, cols], cw_ref[1:2, cols], cw_ref[2:3, cols]
    b = cb_ref[:, cols]
    r = [stage[pl.ds(SUBLANES - 2 + k, half, stride=2), :] for k in range(4)]
    even = r[2] * w2 + r[1] * w1 + r[0] * w0 + b
    odd = r[3] * w2 + r[2] * w1 + r[1] * w0 + b
    return even, odd


def _up_kernel(h_ref, wg_ref, wu_ref, cwg_ref, cwu_ref, cbg_ref, cbu_ref, o_ref,
               tailg_ref, tailu_ref, stage_ref, act_ref):
    @pl.when(pl.program_id(1) == 0)
    def _():
        tailg_ref[...] = jnp.zeros_like(tailg_ref)
        tailu_ref[...] = jnp.zeros_like(tailu_ref)

    tm, n = o_ref.shape
    half = tm // 2
    for c in range(n // SUB_N):
        cols = slice(c * SUB_N, (c + 1) * SUB_N)
        g = jnp.dot(h_ref[...], wg_ref[:, cols], preferred_element_type=F32)
        u = jnp.dot(h_ref[...], wu_ref[:, cols], preferred_element_type=F32)
        for lh in range(SUB_N // LANES):
            lanes = slice(c * SUB_N + lh * LANES, c * SUB_N + (lh + 1) * LANES)
            sg = stage_ref.at[c % 2, 0, lh]
            su = stage_ref.at[c % 2, 1, lh]
            sg[0:SUBLANES, :] = tailg_ref[:, lanes]
            su[0:SUBLANES, :] = tailu_ref[:, lanes]
            sg[SUBLANES:, :] = g[:, lh * LANES:(lh + 1) * LANES]
            su[SUBLANES:, :] = u[:, lh * LANES:(lh + 1) * LANES]
            tailg_ref[:, lanes] = sg[tm:, :]
            tailu_ref[:, lanes] = su[tm:, :]
            plane = act_ref.at[lh]
            gates = _conv_rows(sg, half, lanes, cwg_ref, cbg_ref)
            ups = _conv_rows(su, half, lanes, cwu_ref, cbu_ref)
            for parity in range(2):
                plane[pl.ds(parity, half, stride=2), :] = _silu(gates[parity]) * ups[parity]
            o_ref[:, lanes] = plane[...].astype(o_ref.dtype)


def _ffn_up(h2, w_up_bf16, conv_w, conv_b, tm=512, tn=D_FF // 2):
    s, d = h2.shape
    nj = D_FF // tn
    once = pl.Buffered(1)
    return pl.pallas_call(
        _up_kernel,
        out_shape=jax.ShapeDtypeStruct((s, D_FF), BF16),
        grid=(nj, s // tm),
        in_specs=[pl.BlockSpec((tm, d), lambda j, i: (i, 0)),
                  pl.BlockSpec((d, tn), lambda j, i: (0, j), pipeline_mode=once),
                  pl.BlockSpec((d, tn), lambda j, i: (0, nj + j), pipeline_mode=once),
                  pl.BlockSpec((CONV_WIDTH, tn), lambda j, i: (0, j)),
                  pl.BlockSpec((CONV_WIDTH, tn), lambda j, i: (0, nj + j)),
                  pl.BlockSpec((1, tn), lambda j, i: (0, j)),
                  pl.BlockSpec((1, tn), lambda j, i: (0, nj + j))],
        out_specs=pl.BlockSpec((tm, tn), lambda j, i: (i, j)),
        scratch_shapes=[pltpu.VMEM((SUBLANES, tn), F32),
                        pltpu.VMEM((SUBLANES, tn), F32),
                        pltpu.VMEM((2, 2, SUB_N // LANES, SUBLANES + tm, LANES), F32),
                        pltpu.VMEM((SUB_N // LANES, tm, LANES), F32)],
        compiler_params=_params("parallel", "arbitrary"),
        name="ffn_up",
    )(h2, w_up_bf16, w_up_bf16, conv_w, conv_w, conv_b, conv_b)


def _down_kernel(a_ref, w_ref, x1_ref, g_ref, fw_ref, o_ref):
    y = jnp.dot(a_ref[...], w_ref[...], preferred_element_type=F32)
    x2 = x1_ref[...] + g_ref[...] * y
    ms = jnp.mean(x2 * x2, axis=-1, keepdims=True)
    o_ref[...] = (x2 * lax.rsqrt(ms + EPS)) * fw_ref[...]


def _ffn_down(act, w_down_bf16, x1, mod, final_w, tm=256):
    s, d = x1.shape
    return pl.pallas_call(
        _down_kernel,
        out_shape=jax.ShapeDtypeStruct((s, d), F32),
        grid=(s // tm,),
        in_specs=[pl.BlockSpec((tm, D_FF), lambda i: (i, 0)),
                  pl.BlockSpec((D_FF, d), lambda i: (0, 0), pipeline_mode=pl.Buffered(1)),
                  pl.BlockSpec((tm, d), lambda i: (i, 0)),
                  pl.BlockSpec((1, d), lambda i: (0, 5)),
                  pl.BlockSpec((1, d), lambda i: (0, 0))],
        out_specs=pl.BlockSpec((tm, d), lambda i: (i, 0)),
        compiler_params=_params("parallel"),
        name="ffn_down",
    )(act, w_down_bf16, x1, mod, final_w)


def _rotary_inv_freq():
    inv = 1.0 / (ROPE_THETA ** (jnp.arange(0, ROT_DIM, 2, dtype=F32) / ROT_DIM))
    pad = jnp.zeros((HEAD_DIM - ROT_DIM,), F32)
    return jnp.concatenate([inv, inv, pad]).reshape(1, HEAD_DIM)


def kernel(x, c, positions, w_ada, b_ada, norm1_w, w_in, lb_logits, gnorm_w, w_out,
           norm2_w, w_up, conv_w, conv_b, w_down, final_norm_w):
    b, s, d = x.shape
    assert b == 1 and d == D_MODEL and w_ada.shape[0] == 1
    x2d = x.reshape(s, d)
    pos_col = positions.reshape(s, 1)

    mod = _modulation(c.reshape(d, 1), w_ada[0], b_ada)
    proj = _in_proj(x2d, pos_col, _rotary_inv_freq(), norm1_w, mod, mod, lb_logits,
                    _arrange_w_in(w_in[0]))
    y_attn = _attention(proj)
    y_rec = _recurrence(proj, gnorm_w)
    x1, h2 = _out_proj(x2d, y_attn, y_rec, w_out[0].astype(BF16), mod, norm2_w)
    act = _ffn_up(h2, w_up[0].astype(BF16), conv_w[0], conv_b)
    out = _ffn_down(act, w_down[0].astype(BF16), x1, mod, final_norm_w.reshape(1, d))
    return out.reshape(b, s, d)
```

```python
import jax
import jax.numpy as jnp
from jax import lax
from jax.experimental import pallas as pl
from jax.experimental.pallas import tpu as pltpu

F32 = jnp.float32
BF16 = jnp.bfloat16

D_MODEL = 2048
N_HEADS = 8
HEAD_DIM = 128
WIDTH = N_HEADS * HEAD_DIM
ROT_DIM = HEAD_DIM // 4
ROPE_THETA = 500000.0
DILATIONS = (1, 4, 16)
WINDOW_STEPS = 128
D_FF = 5632
CONV_WIDTH = 3
EPS = 1e-6
MASKED = -1e30
LOG2_E = 1.4426950408889634

VMEM_LIMIT = 56 * 1024 * 1024
SUB_N = 256
LANES = 128
SUBLANES = 8


def _sigmoid(x):
    return 1.0 / (1.0 + jnp.exp(-x))


def _silu(x):
    return x * _sigmoid(x)


def _params(*semantics):
    return pltpu.CompilerParams(dimension_semantics=semantics,
                                vmem_limit_bytes=VMEM_LIMIT)


def _mod_kernel(c_ref, w_ref, b_ref, o_ref):
    ca = _silu(c_ref[...])
    o_ref[...] = jnp.sum(w_ref[...] * ca, axis=0, keepdims=True) + b_ref[...]


def _modulation(c_col, w_ada, b_ada):
    d, n = w_ada.shape
    tn = 512
    return pl.pallas_call(
        _mod_kernel,
        out_shape=jax.ShapeDtypeStruct((1, n), F32),
        grid=(n // tn,),
        in_specs=[pl.BlockSpec((d, 1), lambda j: (0, 0)),
                  pl.BlockSpec((d, tn), lambda j: (0, j)),
                  pl.BlockSpec((1, tn), lambda j: (0, j))],
        out_specs=pl.BlockSpec((1, tn), lambda j: (0, j)),
        compiler_params=_params("parallel"),
        name="mod",
    )(c_col, w_ada, b_ada)


def _norm_modulate(x, norm_w, scale, shift):
    ms = jnp.mean(x * x, axis=-1, keepdims=True)
    return (x * lax.rsqrt(ms + EPS)) * norm_w * (1.0 + scale) + shift


SEG_AQ, SEG_AV, SEG_AK, SEG_HI, SEG_HF, SEG_HQ, SEG_HG = range(7)
SEG_SOURCE = (0, 2, 1, 5, 4, 3, 6)


def _in_kernel(x_ref, pos_ref, invf_ref, nw_ref, sh_ref, sc_ref, lbl_ref, w_ref,
               o_ref, h_ref, cos_ref, sina_ref, sinb_ref):
    j = pl.program_id(1)

    @pl.when(j == 0)
    def _():
        h = _norm_modulate(x_ref[...], nw_ref[...], sc_ref[...], sh_ref[...])
        h_ref[...] = h.astype(BF16)
        tm = pos_ref.shape[0]
        groups = LANES // ROT_DIM
        rows = tm // groups
        lane = lax.broadcasted_iota(jnp.int32, (rows, LANES), 1)
        posf = pos_ref[...].astype(F32)
        packed = jnp.broadcast_to(posf[(groups - 1) * rows:], (rows, LANES))
        for g in range(groups - 2, -1, -1):
            packed = jnp.where(lane < (g + 1) * ROT_DIM,
                               jnp.broadcast_to(posf[g * rows:(g + 1) * rows], (rows, LANES)),
                               packed)
        ang = packed * invf_ref[...]
        c = jnp.cos(ang)
        s = jnp.sin(ang)
        half = ROT_DIM // 2
        for g in range(groups):
            cg = pltpu.roll(c, (LANES - g * ROT_DIM) % LANES, 1) if g else c
            sg = pltpu.roll(s, (LANES - g * ROT_DIM) % LANES, 1) if g else s
            dst = slice(g * rows, (g + 1) * rows)
            cos_ref[dst, :] = jnp.where(lane < ROT_DIM, cg, 1.0)
            sina_ref[dst, :] = jnp.where(lane < half, -sg, 0.0)
            sinb_ref[dst, :] = jnp.where((lane >= half) & (lane < ROT_DIM), sg, 0.0)

    def segment(seg, epilogue):
        half = seg % 2
        for c0 in range(0, WIDTH, SUB_N):
            cols = slice(c0, c0 + SUB_N)
            r = jnp.dot(h_ref[...], w_ref[:, seg * WIDTH + c0:seg * WIDTH + c0 + SUB_N],
                        preferred_element_type=F32)
            o_ref[:, half * WIDTH + c0:half * WIDTH + c0 + SUB_N] = epilogue(r, cols)

    def rotary(scale):
        half = ROT_DIM // 2

        def epilogue(r, cols):
            outs = []
            for hh in range(SUB_N // HEAD_DIM):
                rh = r[:, hh * HEAD_DIM:(hh + 1) * HEAD_DIM]
                out = (rh * cos_ref[...]
                       + pltpu.roll(rh, HEAD_DIM - half, 1) * sina_ref[...]
                       + pltpu.roll(rh, half, 1) * sinb_ref[...])
                outs.append(out * scale if scale != 1.0 else out)
            return jnp.concatenate(outs, axis=1)
        return epilogue

    def plain(r, cols):
        return r

    @pl.when(j == 0)
    def _():
        segment(SEG_AQ, rotary(HEAD_DIM ** -0.5 * LOG2_E))
        segment(SEG_AV, plain)

    @pl.when(j == 1)
    def _():
        segment(SEG_AK, rotary(1.0))
        segment(SEG_HI, plain)

    @pl.when(j == 2)
    def _():
        lg = lbl_ref[...]
        e = jnp.exp(lg - jnp.max(lg, axis=0, keepdims=True))
        lb = e[0:1, :] / jnp.sum(e, axis=0, keepdims=True)

        def log2_forget(r, cols):
            f = lb[:, cols] + (1.0 - lb[:, cols]) * _sigmoid(r)
            return jnp.log2(f)
        segment(SEG_HF, log2_forget)
        segment(SEG_HQ, lambda r, cols: _silu(r) * (HEAD_DIM ** -0.5))

    @pl.when(j == 3)
    def _():
        segment(SEG_HG, lambda r, cols: _silu(r))
        o_ref[:, WIDTH:] = jnp.zeros((o_ref.shape[0], WIDTH), F32)


def _arrange_w_in(w_in):
    segs = [w_in[:, k * WIDTH:(k + 1) * WIDTH] for k in SEG_SOURCE]
    return jnp.concatenate(segs, axis=1).astype(BF16)


def _in_proj(x2d, pos_col, invf, norm_w, shift, scale, lb_logits, w_in_bf16, tm=512):
    s, d = x2d.shape
    n = w_in_bf16.shape[1]
    steps = pl.cdiv(n, 2 * WIDTH)
    return pl.pallas_call(
        _in_kernel,
        out_shape=jax.ShapeDtypeStruct((s, steps * 2 * WIDTH), F32),
        grid=(s // tm, steps),
        in_specs=[pl.BlockSpec((tm, d), lambda i, j: (i, 0)),
                  pl.BlockSpec((tm, 1), lambda i, j: (i, 0)),
                  pl.BlockSpec((1, HEAD_DIM), lambda i, j: (0, 0)),
                  pl.BlockSpec((1, d), lambda i, j: (0, 0)),
                  pl.BlockSpec((1, d), lambda i, j: (0, 0)),
                  pl.BlockSpec((1, d), lambda i, j: (0, 1)),
                  pl.BlockSpec(lb_logits.shape, lambda i, j: (0, 0)),
                  pl.BlockSpec((d, n), lambda i, j: (0, 0), pipeline_mode=pl.Buffered(1))],
        out_specs=pl.BlockSpec((tm, 2 * WIDTH), lambda i, j: (i, j)),
        scratch_shapes=[pltpu.VMEM((tm, d), BF16),
                        pltpu.VMEM((tm, HEAD_DIM), F32),
                        pltpu.VMEM((tm, HEAD_DIM), F32),
                        pltpu.VMEM((tm, HEAD_DIM), F32)],
        compiler_params=_params("parallel", "arbitrary"),
        name="in_proj",
    )(x2d, pos_col, invf, norm_w, shift, scale, lb_logits, w_in_bf16)


ATT_TILE = 2048
ATT_BLK = 128
ATT_P4 = 4


def _attn_kernel(q_ref, k_ref, v_ref, o_ref,
                 q4_ref, k4_ref, v4_ref, k16_ref, v16_ref, ktail_ref, vtail_ref,
                 acc_ref, m_ref, l_ref, y_ref):
    t = pl.program_id(1)
    cur = t % 2
    prev = 1 - cur
    span4 = ATT_TILE // ATT_P4

    @pl.when(t == 0)
    def _():
        k4_ref[1] = jnp.zeros_like(k4_ref[1])
        v4_ref[1] = jnp.zeros_like(v4_ref[1])
        k16_ref[1] = jnp.zeros_like(k16_ref[1])
        v16_ref[1] = jnp.zeros_like(v16_ref[1])
        ktail_ref[...] = jnp.zeros_like(ktail_ref)
        vtail_ref[...] = jnp.zeros_like(vtail_ref)

    qi = lax.broadcasted_iota(jnp.int32, (ATT_BLK, 2 * ATT_BLK), 0)
    ki = lax.broadcasted_iota(jnp.int32, (ATT_BLK, 2 * ATT_BLK), 1)
    band = (ki >= qi) & (ki <= qi + WINDOW_STEPS)
    band_first = band & ((ki >= ATT_BLK) | (t > 0))
    nt_dims = (((1,), (1,)), ((), ()))
    ones = jnp.ones((2 * ATT_BLK, HEAD_DIM), BF16)

    def block(qb, kk, vv, valid):
        s = lax.dot_general(qb, kk, nt_dims, preferred_element_type=F32)
        s = jnp.where(valid, s, MASKED)
        m = jnp.max(s, axis=1, keepdims=True)
        pe = jnp.exp2(s - m).astype(BF16)
        ext = jnp.dot(pe, jnp.concatenate([vv, ones], axis=1), preferred_element_type=F32)
        return (ext[:, :HEAD_DIM], jnp.broadcast_to(m, (ATT_BLK, HEAD_DIM)),
                ext[:, HEAD_DIM:])

    def put(p, rows, res):
        acc_ref[p, rows, :], m_ref[p, rows, :], l_ref[p, rows, :] = res

    for r in range(ATT_P4):
        src = pl.ds(r, span4, stride=ATT_P4)
        dst = pl.ds(r * span4, span4)
        q4_ref[dst, :] = q_ref[src, :]
        k4_ref[cur, dst, :] = k_ref[src, :]
        v4_ref[cur, dst, :] = v_ref[src, :]

    for b in range(ATT_TILE // ATT_BLK):
        rows = pl.ds(b * ATT_BLK, ATT_BLK)
        qb = q_ref[rows, :].astype(BF16)
        if b > 0:
            both = pl.ds((b - 1) * ATT_BLK, 2 * ATT_BLK)
            kk = k_ref[both, :].astype(BF16)
            vv = v_ref[both, :].astype(BF16)
            valid = band
        else:
            kk = jnp.concatenate([ktail_ref[...], k_ref[rows, :].astype(BF16)], axis=0)
            vv = jnp.concatenate([vtail_ref[...], v_ref[rows, :].astype(BF16)], axis=0)
            valid = band_first
        put(0, rows, block(qb, kk, vv, valid))
    last = pl.ds(ATT_TILE - ATT_BLK, ATT_BLK)
    ktail_ref[...] = k_ref[last, :].astype(BF16)
    vtail_ref[...] = v_ref[last, :].astype(BF16)

    for r in range(ATT_P4):
        for b in range(span4 // ATT_BLK):
            base = r * span4 + b * ATT_BLK
            rows = pl.ds(base, ATT_BLK)
            qb = q4_ref[rows, :].astype(BF16)
            if b > 0:
                both = pl.ds(base - ATT_BLK, 2 * ATT_BLK)
                kk = k4_ref[cur, both, :].astype(BF16)
                vv = v4_ref[cur, both, :].astype(BF16)
                valid = band
            else:
                back = pl.ds((r + 1) * span4 - ATT_BLK, ATT_BLK)
                kk = jnp.concatenate([k4_ref[prev, back, :].astype(BF16),
                                      k4_ref[cur, rows, :].astype(BF16)], axis=0)
                vv = jnp.concatenate([v4_ref[prev, back, :].astype(BF16),
                                      v4_ref[cur, rows, :].astype(BF16)], axis=0)
                valid = band_first
            put(1, rows, block(qb, kk, vv, valid))

    for r in range(ATT_P4):
        for r2 in range(ATT_P4):
            src = pl.ds(r * span4 + r2, ATT_BLK, stride=ATT_P4)
            dst = pl.ds((ATT_P4 * r2 + r) * ATT_BLK, ATT_BLK)
            qb = q4_ref[src, :].astype(BF16)
            kc = k4_ref[cur, src, :].astype(BF16)
            vc = v4_ref[cur, src, :].astype(BF16)
            k16_ref[cur, dst, :] = kc
            v16_ref[cur, dst, :] = vc
            kk = jnp.concatenate([k16_ref[prev, dst, :], kc], axis=0)
            vv = jnp.concatenate([v16_ref[prev, dst, :], vc], axis=0)
            put(2, src, block(qb, kk, vv, band_first))

    for r in range(ATT_P4):
        for b in range(span4 // ATT_BLK):
            rows4 = pl.ds(r * span4 + b * ATT_BLK, ATT_BLK)
            nat = pl.ds(r + ATT_P4 * b * ATT_BLK, ATT_BLK, stride=ATT_P4)
            ms = [m_ref[0, nat, :], m_ref[1, rows4, :], m_ref[2, rows4, :]]
            ls = [l_ref[0, nat, :], l_ref[1, rows4, :], l_ref[2, rows4, :]]
            accs = [acc_ref[0, nat, :], acc_ref[1, rows4, :], acc_ref[2, rows4, :]]
            m_all = jnp.maximum(jnp.maximum(ms[0], ms[1]), ms[2])
            num = jnp.zeros((ATT_BLK, HEAD_DIM), F32)
            den = jnp.zeros((ATT_BLK, HEAD_DIM), F32)
            for p in range(len(DILATIONS)):
                w = jnp.exp2(ms[p] - m_all)
                num = num + w * accs[p]
                den = den + w * ls[p]
            y_ref[nat, :] = num / den
    o_ref[...] = y_ref[...].astype(o_ref.dtype)


def _attention(proj):
    s = proj.shape[0]
    nt = s // ATT_TILE
    blk = (ATT_TILE, HEAD_DIM)
    n_pat = len(DILATIONS)
    return pl.pallas_call(
        _attn_kernel,
        out_shape=jax.ShapeDtypeStruct((s, WIDTH), BF16),
        grid=(N_HEADS, nt),
        in_specs=[pl.BlockSpec(blk, lambda h, t: (t, SEG_AQ * N_HEADS + h)),
                  pl.BlockSpec(blk, lambda h, t: (t, SEG_AK * N_HEADS + h)),
                  pl.BlockSpec(blk, lambda h, t: (t, SEG_AV * N_HEADS + h))],
        out_specs=pl.BlockSpec(blk, lambda h, t: (t, h)),
        scratch_shapes=[pltpu.VMEM(blk, F32),
                        pltpu.VMEM((2,) + blk, F32),
                        pltpu.VMEM((2,) + blk, F32),
                        pltpu.VMEM((2,) + blk, BF16),
                        pltpu.VMEM((2,) + blk, BF16),
                        pltpu.VMEM((ATT_BLK, HEAD_DIM), BF16),
                        pltpu.VMEM((ATT_BLK, HEAD_DIM), BF16),
                        pltpu.VMEM((n_pat,) + blk, F32),
                        pltpu.VMEM((n_pat,) + blk, F32),
                        pltpu.VMEM((n_pat,) + blk, F32),
                        pltpu.VMEM(blk, F32)],
        compiler_params=_params("parallel", "arbitrary"),
        name="attn",
    )(proj, proj, proj)


REC_CHUNK = 128
REC_LEVELS = 7
REC_ROWS = 1024
REC_HEADS = 4


def _block_scans(lf):
    c = lf.shape[0]
    fwd, rev = [lf], [lf]
    x, y = lf, lf
    for j in range(REC_LEVELS):
        m = 1 << j
        if m < SUBLANES:
            x3 = x.reshape(c // SUBLANES, SUBLANES, HEAD_DIM)
            y3 = y.reshape(c // SUBLANES, SUBLANES, HEAD_DIM)
            sub = lax.broadcasted_iota(jnp.int32, x3.shape, 1)
            block0 = sub & ~(2 * m - 1)
            upper = (sub & m) != 0
            addx = jnp.where(upper, jnp.take_along_axis(x3, block0 + (m - 1), axis=1), 0.0)
            addy = jnp.where(upper, 0.0, jnp.take_along_axis(y3, block0 + m, axis=1))
            x = x + addx.reshape(c, HEAD_DIM)
            y = y + addy.reshape(c, HEAD_DIM)
        else:
            xs, ys = [], []
            for b in range(c // (2 * m)):
                lo = x[b * 2 * m: b * 2 * m + m]
                hi = x[b * 2 * m + m: (b + 1) * 2 * m]
                xs += [lo, hi + lo[m - 1:m, :]]
                lo = y[b * 2 * m: b * 2 * m + m]
                hi = y[b * 2 * m + m: (b + 1) * 2 * m]
                ys += [lo + hi[0:1, :], hi]
            x = jnp.concatenate(xs, axis=0)
            y = jnp.concatenate(ys, axis=0)
        fwd.append(x)
        rev.append(y)
    return fwd, rev


def _rec_kernel(q_ref, lf_ref, v_ref, g_ref, gw_ref, o_ref, state_ref):
    @pl.when(pl.program_id(1) == 0)
    def _():
        state_ref[...] = jnp.zeros_like(state_ref)

    c = REC_CHUNK
    ti = lax.broadcasted_iota(jnp.int32, (c, c), 0)
    si = lax.broadcasted_iota(jnp.int32, (c, c), 1)
    diff = ti ^ si
    level = jnp.full((c, c), -1, jnp.int32)
    for j in range(REC_LEVELS):
        level = jnp.where((diff >> j) == 1, j, level)
    level = jnp.where(si > ti, -1, jnp.where(si == ti, REC_LEVELS, level))
    nt_dims = (((1,), (1,)), ((), ()))

    def one_head(hh, rows):
        lanes = slice(hh * HEAD_DIM, (hh + 1) * HEAD_DIM)
        qf = q_ref[rows, lanes]
        lf = lf_ref[rows, lanes]
        v = v_ref[rows, lanes].astype(BF16)
        kk = 1.0 - jnp.exp2(lf)
        fwd, rev = _block_scans(lf)
        b = fwd[REC_LEVELS]

        a = jnp.where(level == REC_LEVELS,
                      lax.dot_general(qf.astype(BF16), kk.astype(BF16), nt_dims,
                                      preferred_element_type=F32), 0.0)
        for j in range(REC_LEVELS):
            ql = (qf * jnp.exp2(fwd[j])).astype(BF16)
            kl = (kk * jnp.exp2(rev[j] - lf)).astype(BF16)
            al = lax.dot_general(ql, kl, nt_dims, preferred_element_type=F32)
            a = jnp.where(level == j, al, a)

        state_t = state_ref[hh]
        o = (lax.dot_general((qf * jnp.exp2(b)).astype(BF16), state_t.astype(BF16),
                             nt_dims, preferred_element_type=F32)
             + jnp.dot(a.astype(BF16), v, preferred_element_type=F32))
        b_last = b[c - 1:c, :]
        kd = (kk * jnp.exp2(b_last - b)).astype(BF16)
        upd_t = lax.dot_general(v, kd, (((0,), (0,)), ((), ())),
                                preferred_element_type=F32)
        state_ref[hh] = jnp.exp2(b_last) * state_t + upd_t

        o = o * lax.rsqrt(jnp.mean(o * o, axis=-1, keepdims=True) + EPS) * gw_ref[...]
        o_ref[rows, lanes] = (o * g_ref[rows, lanes]).astype(o_ref.dtype)

    def chunk(ci, carry):
        rows = pl.ds(pl.multiple_of(ci * c, c), c)
        for hh in range(REC_HEADS):
            one_head(hh, rows)
        return carry

    lax.fori_loop(0, REC_ROWS // c, chunk, 0, unroll=2)


def _recurrence(proj, gnorm_w):
    s = proj.shape[0]
    blk = (REC_ROWS, REC_HEADS * HEAD_DIM)
    groups = N_HEADS // REC_HEADS
    seg = lambda k: (lambda h, i: (i, k * groups + h))
    return pl.pallas_call(
        _rec_kernel,
        out_shape=jax.ShapeDtypeStruct((s, WIDTH), BF16),
        grid=(groups, s // REC_ROWS),
        in_specs=[pl.BlockSpec(blk, seg(SEG_HQ)),
                  pl.BlockSpec(blk, seg(SEG_HF)),
                  pl.BlockSpec(blk, seg(SEG_HI)),
                  pl.BlockSpec(blk, seg(SEG_HG)),
                  pl.BlockSpec((1, HEAD_DIM), lambda h, i: (0, 0))],
        out_specs=pl.BlockSpec(blk, lambda h, i: (i, h)),
        scratch_shapes=[pltpu.VMEM((REC_HEADS, HEAD_DIM, HEAD_DIM), F32)],
        compiler_params=_params("parallel", "arbitrary"),
        name="hgrn",
    )(proj, proj, proj, proj, gnorm_w)


def _out_kernel(x_ref, ya_ref, yr_ref, w_ref, g_ref, nw_ref, sh_ref, sc_ref,
                x1_ref, h2_ref):
    y = (jnp.dot(ya_ref[...], w_ref[0:WIDTH, :], preferred_element_type=F32)
         + jnp.dot(yr_ref[...], w_ref[WIDTH:, :], preferred_element_type=F32))
    x1 = x_ref[...] + g_ref[...] * y
    x1_ref[...] = x1
    h2_ref[...] = _norm_modulate(x1, nw_ref[...], sc_ref[...], sh_ref[...]).astype(BF16)


def _out_proj(x2d, y_attn, y_rec, w_out_bf16, mod, norm2_w, tm=256):
    s, d = x2d.shape
    vec = lambda k: pl.BlockSpec((1, d), lambda i: (0, k))
    return pl.pallas_call(
        _out_kernel,
        out_shape=(jax.ShapeDtypeStruct((s, d), F32),
                   jax.ShapeDtypeStruct((s, d), BF16)),
        grid=(s // tm,),
        in_specs=[pl.BlockSpec((tm, d), lambda i: (i, 0)),
                  pl.BlockSpec((tm, WIDTH), lambda i: (i, 0)),
                  pl.BlockSpec((tm, WIDTH), lambda i: (i, 0)),
                  pl.BlockSpec((2 * WIDTH, d), lambda i: (0, 0)),
                  vec(2),
                  pl.BlockSpec((1, d), lambda i: (0, 0)),
                  vec(3), vec(4)],
        out_specs=(pl.BlockSpec((tm, d), lambda i: (i, 0)),
                   pl.BlockSpec((tm, d), lambda i: (i, 0))),
        compiler_params=_params("parallel"),
        name="out_proj",
    )(x2d, y_attn, y_rec, w_out_bf16, mod, norm2_w, mod, mod)


def _conv_rows(stage, half, cols, cw_ref, cb_ref):
    w0, w1, w2 = cw_ref[0:1, cols], cw_ref[1:2, cols], cw_ref[2:3, cols]
    b = cb_ref[:, cols]
    r = [stage[pl.ds(SUBLANES - 2 + k, half, stride=2), :] for k in range(4)]
    even = r[2] * w2 + r[1] * w1 + r[0] * w0 + b
    odd = r[3] * w2 + r[2] * w1 + r[1] * w0 + b
    return even, odd


def _up_kernel(h_ref, wg_ref, wu_ref, cwg_ref, cwu_ref, cbg_ref, cbu_ref, o_ref,
               tailg_ref, tailu_ref, stage_ref, act_ref):
    @pl.when(pl.program_id(1) == 0)
    def _():
        tailg_ref[...] = jnp.zeros_like(tailg_ref)
        tailu_ref[...] = jnp.zeros_like(tailu_ref)

    tm, n = o_ref.shape
    half = tm // 2
    for c in range(n // SUB_N):
        cols = slice(c * SUB_N, (c + 1) * SUB_N)
        g = jnp.dot(h_ref[...], wg_ref[:, cols], preferred_element_type=F32)
        u = jnp.dot(h_ref[...], wu_ref[:, cols], preferred_element_type=F32)
        for lh in range(SUB_N // LANES):
            lanes = slice(c * SUB_N + lh * LANES, c * SUB_N + (lh + 1) * LANES)
            sg = stage_ref.at[c % 2, 0, lh]
            su = stage_ref.at[c % 2, 1, lh]
            sg[0:SUBLANES, :] = tailg_ref[:, lanes]
            su[0:SUBLANES, :] = tailu_ref[:, lanes]
            sg[SUBLANES:, :] = g[:, lh * LANES:(lh + 1) * LANES]
            su[SUBLANES:, :] = u[:, lh * LANES:(lh + 1) * LANES]
            tailg_ref[:, lanes] = sg[tm:, :]
            tailu_ref[:, lanes] = su[tm:, :]
            plane = act_ref.at[lh]
            gates = _conv_rows(sg, half, lanes, cwg_ref, cbg_ref)
            ups = _conv_rows(su, half, lanes, cwu_ref, cbu_ref)
            for parity in range(2):
                plane[pl.ds(parity, half, stride=2), :] = _silu(gates[parity]) * ups[parity]
            o_ref[:, lanes] = plane[...].astype(o_ref.dtype)


def _ffn_up(h2, w_up_bf16, conv_w, conv_b, tm=512, tn=D_FF // 2):
    s, d = h2.shape
    nj = D_FF // tn
    once = pl.Buffered(1)
    return pl.pallas_call(
        _up_kernel,
        out_shape=jax.ShapeDtypeStruct((s, D_FF), BF16),
        grid=(nj, s // tm),
        in_specs=[pl.BlockSpec((tm, d), lambda j, i: (i, 0)),
                  pl.BlockSpec((d, tn), lambda j, i: (0, j), pipeline_mode=once),
                  pl.BlockSpec((d, tn), lambda j, i: (0, nj + j), pipeline_mode=once),
                  pl.BlockSpec((CONV_WIDTH, tn), lambda j, i: (0, j)),
                  pl.BlockSpec((CONV_WIDTH, tn), lambda j, i: (0, nj + j)),
                  pl.BlockSpec((1, tn), lambda j, i: (0, j)),
                  pl.BlockSpec((1, tn), lambda j, i: (0, nj + j))],
        out_specs=pl.BlockSpec((tm, tn), lambda j, i: (i, j)),
        scratch_shapes=[pltpu.VMEM((SUBLANES, tn), F32),
                        pltpu.VMEM((SUBLANES, tn), F32),
                        pltpu.VMEM((2, 2, SUB_N // LANES, SUBLANES + tm, LANES), F32),
                        pltpu.VMEM((SUB_N // LANES, tm, LANES), F32)],
        compiler_params=_params("parallel", "arbitrary"),
        name="ffn_up",
    )(h2, w_up_bf16, w_up_bf16, conv_w, conv_w, conv_b, conv_b)


def _down_kernel(a_ref, w_ref, x1_ref, g_ref, fw_ref, o_ref):
    y = jnp.dot(a_ref[...], w_ref[...], preferred_element_type=F32)
    x2 = x1_ref[...] + g_ref[...] * y
    ms = jnp.mean(x2 * x2, axis=-1, keepdims=True)
    o_ref[...] = (x2 * lax.rsqrt(ms + EPS)) * fw_ref[...]


def _ffn_down(act, w_down_bf16, x1, mod, final_w, tm=256):
    s, d = x1.shape
    return pl.pallas_call(
        _down_kernel,
        out_shape=jax.ShapeDtypeStruct((s, d), F32),
        grid=(s // tm,),
        in_specs=[pl.BlockSpec((tm, D_FF), lambda i: (i, 0)),
                  pl.BlockSpec((D_FF, d), lambda i: (0, 0), pipeline_mode=pl.Buffered(1)),
                  pl.BlockSpec((tm, d), lambda i: (i, 0)),
                  pl.BlockSpec((1, d), lambda i: (0, 5)),
                  pl.BlockSpec((1, d), lambda i: (0, 0))],
        out_specs=pl.BlockSpec((tm, d), lambda i: (i, 0)),
        compiler_params=_params("parallel"),
        name="ffn_down",
    )(act, w_down_bf16, x1, mod, final_w)


def _rotary_inv_freq():
    inv = 1.0 / (ROPE_THETA ** (jnp.arange(0, ROT_DIM, 2, dtype=F32) / ROT_DIM))
    return jnp.tile(jnp.concatenate([inv, inv]), LANES // ROT_DIM).reshape(1, LANES)


def kernel(x, c, positions, w_ada, b_ada, norm1_w, w_in, lb_logits, gnorm_w, w_out,
           norm2_w, w_up, conv_w, conv_b, w_down, final_norm_w):
    b, s, d = x.shape
    assert b == 1 and d == D_MODEL and w_ada.shape[0] == 1
    x2d = x.reshape(s, d)
    pos_col = positions.reshape(s, 1)

    mod = _modulation(c.reshape(d, 1), w_ada[0], b_ada)
    proj = _in_proj(x2d, pos_col, _rotary_inv_freq(), norm1_w, mod, mod, lb_logits,
                    _arrange_w_in(w_in[0]))
    y_attn = _attention(proj)
    y_rec = _recurrence(proj, gnorm_w)
    x1, h2 = _out_proj(x2d, y_attn, y_rec, w_out[0].astype(BF16), mod, norm2_w)
    act = _ffn_up(h2, w_up[0].astype(BF16), conv_w[0], conv_b)
    out = _ffn_down(act, w_down[0].astype(BF16), x1, mod, final_norm_w.reshape(1, d))
    return out.reshape(b, s, d)
```

```python
import jax
import jax.numpy as jnp
from jax import lax
from jax.experimental import pallas as pl
from jax.experimental.pallas import tpu as pltpu

F32 = jnp.float32
BF16 = jnp.bfloat16

D_MODEL = 2048
N_HEADS = 8
HEAD_DIM = 128
WIDTH = N_HEADS * HEAD_DIM
ROT_DIM = HEAD_DIM // 4
ROPE_THETA = 500000.0
DILATIONS = (1, 4, 16)
WINDOW_STEPS = 128
D_FF = 5632
CONV_WIDTH = 3
EPS = 1e-6
MASKED = -1e30
LOG2_E = 1.4426950408889634

VMEM_LIMIT = 56 * 1024 * 1024
SUB_N = 256
LANES = 128
SUBLANES = 8


def _sigmoid(x):
    return 1.0 / (1.0 + jnp.exp(-x))


def _silu(x):
    return x * _sigmoid(x)


def _params(*semantics):
    return pltpu.CompilerParams(dimension_semantics=semantics,
                                vmem_limit_bytes=VMEM_LIMIT)


def _mod_kernel(c_ref, w_ref, b_ref, o_ref):
    ca = _silu(c_ref[...])
    o_ref[...] = jnp.sum(w_ref[...] * ca, axis=0, keepdims=True) + b_ref[...]


def _modulation(c_col, w_ada, b_ada):
    d, n = w_ada.shape
    tn = 512
    return pl.pallas_call(
        _mod_kernel,
        out_shape=jax.ShapeDtypeStruct((1, n), F32),
        grid=(n // tn,),
        in_specs=[pl.BlockSpec((d, 1), lambda j: (0, 0)),
                  pl.BlockSpec((d, tn), lambda j: (0, j)),
                  pl.BlockSpec((1, tn), lambda j: (0, j))],
        out_specs=pl.BlockSpec((1, tn), lambda j: (0, j)),
        compiler_params=_params("parallel"),
        name="mod",
    )(c_col, w_ada, b_ada)


def _norm_modulate(x, norm_w, scale, shift):
    ms = jnp.mean(x * x, axis=-1, keepdims=True)
    return (x * lax.rsqrt(ms + EPS)) * norm_w * (1.0 + scale) + shift


SEG_AQ, SEG_AV, SEG_AK, SEG_HI, SEG_HF, SEG_HQ, SEG_HG = range(7)
SEG_SOURCE = (0, 2, 1, 5, 4, 3, 6)


def _in_kernel(x_ref, pos_ref, invf_ref, nw_ref, sh_ref, sc_ref, lbl_ref, w_ref,
               o_ref, h_ref, cos_ref, sina_ref, sinb_ref):
    j = pl.program_id(1)

    @pl.when(j == 0)
    def _():
        h = _norm_modulate(x_ref[...], nw_ref[...], sc_ref[...], sh_ref[...])
        h_ref[...] = h.astype(BF16)
        tm = pos_ref.shape[0]
        groups = LANES // ROT_DIM
        rows = tm // groups
        lane = lax.broadcasted_iota(jnp.int32, (rows, LANES), 1)
        posf = pos_ref[...].astype(F32)
        packed = jnp.broadcast_to(posf[(groups - 1) * rows:], (rows, LANES))
        for g in range(groups - 2, -1, -1):
            packed = jnp.where(lane < (g + 1) * ROT_DIM,
                               jnp.broadcast_to(posf[g * rows:(g + 1) * rows], (rows, LANES)),
                               packed)
        ang = packed * invf_ref[...]
        c = jnp.cos(ang)
        s = jnp.sin(ang)
        half = ROT_DIM // 2
        for g in range(groups):
            cg = pltpu.roll(c, (LANES - g * ROT_DIM) % LANES, 1) if g else c
            sg = pltpu.roll(s, (LANES - g * ROT_DIM) % LANES, 1) if g else s
            dst = slice(g * rows, (g + 1) * rows)
            cos_ref[dst, :] = jnp.where(lane < ROT_DIM, cg, 1.0)
            sina_ref[dst, :] = jnp.where(lane < half, -sg, 0.0)
            sinb_ref[dst, :] = jnp.where((lane >= half) & (lane < ROT_DIM), sg, 0.0)

    def segment(seg, epilogue):
        half = seg % 2
        for c0 in range(0, WIDTH, SUB_N):
            cols = slice(c0, c0 + SUB_N)
            r = jnp.dot(h_ref[...], w_ref[:, seg * WIDTH + c0:seg * WIDTH + c0 + SUB_N],
                        preferred_element_type=F32)
            o_ref[:, half * WIDTH + c0:half * WIDTH + c0 + SUB_N] = epilogue(r, cols)

    def rotary(scale):
        half = ROT_DIM // 2

        def epilogue(r, cols):
            outs = []
            for hh in range(SUB_N // HEAD_DIM):
                rh = r[:, hh * HEAD_DIM:(hh + 1) * HEAD_DIM]
                out = (rh * cos_ref[...]
                       + pltpu.roll(rh, HEAD_DIM - half, 1) * sina_ref[...]
                       + pltpu.roll(rh, half, 1) * sinb_ref[...])
                outs.append(out * scale if scale != 1.0 else out)
            return jnp.concatenate(outs, axis=1)
        return epilogue

    def plain(r, cols):
        return r

    @pl.when(j == 0)
    def _():
        segment(SEG_AQ, rotary(HEAD_DIM ** -0.5 * LOG2_E))
        segment(SEG_AV, plain)

    @pl.when(j == 1)
    def _():
        segment(SEG_AK, rotary(1.0))
        segment(SEG_HI, plain)

    @pl.when(j == 2)
    def _():
        lg = lbl_ref[...]
        e = jnp.exp(lg - jnp.max(lg, axis=0, keepdims=True))
        lb = e[0:1, :] / jnp.sum(e, axis=0, keepdims=True)

        def log2_forget(r, cols):
            f = lb[:, cols] + (1.0 - lb[:, cols]) * _sigmoid(r)
            return jnp.log2(f)
        segment(SEG_HF, log2_forget)
        segment(SEG_HQ, lambda r, cols: _silu(r) * (HEAD_DIM ** -0.5))

    @pl.when(j == 3)
    def _():
        segment(SEG_HG, lambda r, cols: _silu(r))
        o_ref[:, WIDTH:] = jnp.zeros((o_ref.shape[0], WIDTH), F32)


def _arrange_w_in(w_in):
    segs = [w_in[:, k * WIDTH:(k + 1) * WIDTH] for k in SEG_SOURCE]
    return jnp.concatenate(segs, axis=1).astype(BF16)


def _in_proj(x2d, pos_col, invf, norm_w, shift, scale, lb_logits, w_in_bf16, tm=512):
    s, d = x2d.shape
    n = w_in_bf16.shape[1]
    steps = pl.cdiv(n, 2 * WIDTH)
    return pl.pallas_call(
        _in_kernel,
        out_shape=jax.ShapeDtypeStruct((s, steps * 2 * WIDTH), F32),
        grid=(s // tm, steps),
        in_specs=[pl.BlockSpec((tm, d), lambda i, j: (i, 0)),
                  pl.BlockSpec((tm, 1), lambda i, j: (i, 0)),
                  pl.BlockSpec((1, HEAD_DIM), lambda i, j: (0, 0)),
                  pl.BlockSpec((1, d), lambda i, j: (0, 0)),
                  pl.BlockSpec((1, d), lambda i, j: (0, 0)),
                  pl.BlockSpec((1, d), lambda i, j: (0, 1)),
                  pl.BlockSpec(lb_logits.shape, lambda i, j: (0, 0)),
                  pl.BlockSpec((d, n), lambda i, j: (0, 0), pipeline_mode=pl.Buffered(1))],
        out_specs=pl.BlockSpec((tm, 2 * WIDTH), lambda i, j: (i, j)),
        scratch_shapes=[pltpu.VMEM((tm, d), BF16),
                        pltpu.VMEM((tm, HEAD_DIM), F32),
                        pltpu.VMEM((tm, HEAD_DIM), F32),
                        pltpu.VMEM((tm, HEAD_DIM), F32)],
        compiler_params=_params("parallel", "arbitrary"),
        name="in_proj",
    )(x2d, pos_col, invf, norm_w, shift, scale, lb_logits, w_in_bf16)


ATT_TILE = 2048
ATT_BLK = 128
ATT_P4 = 4


def _attn_kernel(q_ref, k_ref, v_ref, o_ref,
                 q4_ref, k4_ref, v4_ref, k16_ref, v16_ref, ktail_ref, vtail_ref,
                 acc_ref, m_ref, l_ref, y_ref):
    t = pl.program_id(1)
    cur = t % 2
    prev = 1 - cur
    span4 = ATT_TILE // ATT_P4

    @pl.when(t == 0)
    def _():
        k4_ref[1] = jnp.zeros_like(k4_ref[1])
        v4_ref[1] = jnp.zeros_like(v4_ref[1])
        k16_ref[1] = jnp.zeros_like(k16_ref[1])
        v16_ref[1] = jnp.zeros_like(v16_ref[1])
        ktail_ref[...] = jnp.zeros_like(ktail_ref)
        vtail_ref[...] = jnp.zeros_like(vtail_ref)

    qi = lax.broadcasted_iota(jnp.int32, (ATT_BLK, 2 * ATT_BLK), 0)
    ki = lax.broadcasted_iota(jnp.int32, (ATT_BLK, 2 * ATT_BLK), 1)
    band = (ki >= qi) & (ki <= qi + WINDOW_STEPS)
    band_first = band & ((ki >= ATT_BLK) | (t > 0))
    nt_dims = (((1,), (1,)), ((), ()))
    ones = jnp.ones((2 * ATT_BLK, HEAD_DIM), BF16)

    def block(qb, kk, vv, valid):
        s = lax.dot_general(qb, kk, nt_dims, preferred_element_type=F32)
        s = jnp.where(valid, s, MASKED)
        m = jnp.max(s, axis=1, keepdims=True)
        pe = jnp.exp2(s - m).astype(BF16)
        ext = jnp.dot(pe, jnp.concatenate([vv, ones], axis=1), preferred_element_type=F32)
        return (ext[:, :HEAD_DIM], jnp.broadcast_to(m, (ATT_BLK, HEAD_DIM)),
                ext[:, HEAD_DIM:])

    def put(p, rows, res):
        acc_ref[p, rows, :], m_ref[p, rows, :], l_ref[p, rows, :] = res

    for r in range(ATT_P4):
        src = pl.ds(r, span4, stride=ATT_P4)
        dst = pl.ds(r * span4, span4)
        q4_ref[dst, :] = q_ref[src, :]
        k4_ref[cur, dst, :] = k_ref[src, :]
        v4_ref[cur, dst, :] = v_ref[src, :]

    for b in range(ATT_TILE // ATT_BLK):
        rows = pl.ds(b * ATT_BLK, ATT_BLK)
        qb = q_ref[rows, :].astype(BF16)
        if b > 0:
            both = pl.ds((b - 1) * ATT_BLK, 2 * ATT_BLK)
            kk = k_ref[both, :].astype(BF16)
            vv = v_ref[both, :].astype(BF16)
            valid = band
        else:
            kk = jnp.concatenate([ktail_ref[...], k_ref[rows, :].astype(BF16)], axis=0)
            vv = jnp.concatenate([vtail_ref[...], v_ref[rows, :].astype(BF16)], axis=0)
            valid = band_first
        put(0, rows, block(qb, kk, vv, valid))
    last = pl.ds(ATT_TILE - ATT_BLK, ATT_BLK)
    ktail_ref[...] = k_ref[last, :].astype(BF16)
    vtail_ref[...] = v_ref[last, :].astype(BF16)

    for r in range(ATT_P4):
        for b in range(span4 // ATT_BLK):
            base = r * span4 + b * ATT_BLK
            rows = pl.ds(base, ATT_BLK)
            qb = q4_ref[rows, :].astype(BF16)
            if b > 0:
                both = pl.ds(base - ATT_BLK, 2 * ATT_BLK)
                kk = k4_ref[cur, both, :].astype(BF16)
                vv = v4_ref[cur, both, :].astype(BF16)
                valid = band
            else:
                back = pl.ds((r + 1) * span4 - ATT_BLK, ATT_BLK)
                kk = jnp.concatenate([k4_ref[prev, back, :].astype(BF16),
                                      k4_ref[cur, rows, :].astype(BF16)], axis=0)
                vv = jnp.concatenate([v4_ref[prev, back, :].astype(BF16),
                                      v4_ref[cur, rows, :].astype(BF16)], axis=0)
                valid = band_first
            put(1, rows, block(qb, kk, vv, valid))

    for r in range(ATT_P4):
        for r2 in range(ATT_P4):
            src = pl.ds(r * span4 + r2, ATT_BLK, stride=ATT_P4)
            dst = pl.ds((ATT_P4 * r2 + r) * ATT_BLK, ATT_BLK)
            qb = q4_ref[src, :].astype(BF16)
            kc = k4_ref[cur, src, :].astype(BF16)
            vc = v4_ref[cur, src, :].astype(BF16)
            k16_ref[cur, dst, :] = kc
            v16_ref[cur, dst, :] = vc
            kk = jnp.concatenate([k16_ref[prev, dst, :], kc], axis=0)
            vv = jnp.concatenate([v16_ref[prev, dst, :], vc], axis=0)
            put(2, src, block(qb, kk, vv, band_first))

    for r in range(ATT_P4):
        for b in range(span4 // ATT_BLK):
            rows4 = pl.ds(r * span4 + b * ATT_BLK, ATT_BLK)
            nat = pl.ds(r + ATT_P4 * b * ATT_BLK, ATT_BLK, stride=ATT_P4)
            ms = [m_ref[0, nat, :], m_ref[1, rows4, :], m_ref[2, rows4, :]]
            ls = [l_ref[0, nat, :], l_ref[1, rows4, :], l_ref[2, rows4, :]]
            accs = [acc_ref[0, nat, :], acc_ref[1, rows4, :], acc_ref[2, rows4, :]]
            m_all = jnp.maximum(jnp.maximum(ms[0], ms[1]), ms[2])
            num = jnp.zeros((ATT_BLK, HEAD_DIM), F32)
            den = jnp.zeros((ATT_BLK, HEAD_DIM), F32)
            for p in range(len(DILATIONS)):
                w = jnp.exp2(ms[p] - m_all)
                num = num + w * accs[p]
                den = den + w * ls[p]
            y_ref[nat, :] = num / den
    o_ref[...] = y_ref[...].astype(o_ref.dtype)


def _attention(proj):
    s = proj.shape[0]
    nt = s // ATT_TILE
    blk = (ATT_TILE, HEAD_DIM)
    n_pat = len(DILATIONS)
    return pl.pallas_call(
        _attn_kernel,
        out_shape=jax.ShapeDtypeStruct((s, WIDTH), BF16),
        grid=(N_HEADS, nt),
        in_specs=[pl.BlockSpec(blk, lambda h, t: (t, SEG_AQ * N_HEADS + h)),
                  pl.BlockSpec(blk, lambda h, t: (t, SEG_AK * N_HEADS + h)),
                  pl.BlockSpec(blk, lambda h, t: (t, SEG_AV * N_HEADS + h))],
        out_specs=pl.BlockSpec(blk, lambda h, t: (t, h)),
        scratch_shapes=[pltpu.VMEM(blk, F32),
                        pltpu.VMEM((2,) + blk, F32),
                        pltpu.VMEM((2,) + blk, F32),
                        pltpu.VMEM((2,) + blk, BF16),
                        pltpu.VMEM((2,) + blk, BF16),
                        pltpu.VMEM((ATT_BLK, HEAD_DIM), BF16),
                        pltpu.VMEM((ATT_BLK, HEAD_DIM), BF16),
                        pltpu.VMEM((n_pat,) + blk, F32),
                        pltpu.VMEM((n_pat,) + blk, F32),
                        pltpu.VMEM((n_pat,) + blk, F32),
                        pltpu.VMEM(blk, F32)],
        compiler_params=_params("parallel", "arbitrary"),
        name="attn",
    )(proj, proj, proj)


REC_CHUNK = 128
REC_LEVELS = 7
REC_ROWS = 1024
REC_HEADS = 4


def _block_scans(lf):
    c = lf.shape[0]
    fwd, rev = [lf], [lf]
    x, y = lf, lf
    for j in range(REC_LEVELS):
        m = 1 << j
        if m < SUBLANES:
            x3 = x.reshape(c // SUBLANES, SUBLANES, HEAD_DIM)
            y3 = y.reshape(c // SUBLANES, SUBLANES, HEAD_DIM)
            sub = lax.broadcasted_iota(jnp.int32, x3.shape, 1)
            block0 = sub & ~(2 * m - 1)
            upper = (sub & m) != 0
            addx = jnp.where(upper, jnp.take_along_axis(x3, block0 + (m - 1), axis=1), 0.0)
            addy = jnp.where(upper, 0.0, jnp.take_along_axis(y3, block0 + m, axis=1))
            x = x + addx.reshape(c, HEAD_DIM)
            y = y + addy.reshape(c, HEAD_DIM)
        else:
            xs, ys = [], []
            for b in range(c // (2 * m)):
                lo = x[b * 2 * m: b * 2 * m + m]
                hi = x[b * 2 * m + m: (b + 1) * 2 * m]
                xs += [lo, hi + lo[m - 1:m, :]]
                lo = y[b * 2 * m: b * 2 * m + m]
                hi = y[b * 2 * m + m: (b + 1) * 2 * m]
                ys += [lo + hi[0:1, :], hi]
            x = jnp.concatenate(xs, axis=0)
            y = jnp.concatenate(ys, axis=0)
        fwd.append(x)
        rev.append(y)
    return fwd, rev


def _rec_kernel(q_ref, lf_ref, v_ref, g_ref, gw_ref, o_ref, state_ref):
    @pl.when(pl.program_id(1) == 0)
    def _():
        state_ref[...] = jnp.zeros_like(state_ref)

    c = REC_CHUNK
    ti = lax.broadcasted_iota(jnp.int32, (c, c), 0)
    si = lax.broadcasted_iota(jnp.int32, (c, c), 1)
    diff = ti ^ si
    level = jnp.full((c, c), -1, jnp.int32)
    for j in range(REC_LEVELS):
        level = jnp.where((diff >> j) == 1, j, level)
    level = jnp.where(si > ti, -1, jnp.where(si == ti, REC_LEVELS, level))
    nt_dims = (((1,), (1,)), ((), ()))

    def one_head(hh, rows):
        lanes = slice(hh * HEAD_DIM, (hh + 1) * HEAD_DIM)
        qf = q_ref[rows, lanes]
        lf = lf_ref[rows, lanes]
        v = v_ref[rows, lanes].astype(BF16)
        kk = 1.0 - jnp.exp2(lf)
        fwd, rev = _block_scans(lf)
        b = fwd[REC_LEVELS]

        a = jnp.where(level == REC_LEVELS,
                      lax.dot_general(qf.astype(BF16), kk.astype(BF16), nt_dims,
                                      preferred_element_type=F32), 0.0)
        for j in range(REC_LEVELS):
            m = 1 << j
            if m < SUBLANES:
                ql = (qf * jnp.exp2(fwd[j])).astype(BF16)
                kl = (kk * jnp.exp2(rev[j] - lf)).astype(BF16)
                al = lax.dot_general(ql, kl, nt_dims, preferred_element_type=F32)
                a = jnp.where(level == j, al, a)
                continue
            lows = [slice(b, b + m) for b in range(0, c, 2 * m)]
            ups = [slice(b + m, b + 2 * m) for b in range(0, c, 2 * m)]
            pick = lambda x, parts: jnp.concatenate([x[p] for p in parts], axis=0)
            ql = (pick(qf, ups) * jnp.exp2(pick(fwd[j], ups))).astype(BF16)
            k_low = pick(kk, lows) * jnp.exp2(pick(rev[j], lows) - pick(lf, lows))
            zeros = jnp.zeros((m, HEAD_DIM), F32)
            kl = jnp.concatenate(
                [piece for b in range(len(lows))
                 for piece in (k_low[b * m:(b + 1) * m], zeros)], axis=0).astype(BF16)
            al = lax.dot_general(ql, kl, nt_dims, preferred_element_type=F32)
            a_up = jnp.where(pick(level, ups) == j, al, pick(a, ups))
            a = jnp.concatenate(
                [piece for b in range(len(lows))
                 for piece in (a[lows[b]], a_up[b * m:(b + 1) * m])], axis=0)

        state_t = state_ref[hh]
        o = (lax.dot_general((qf * jnp.exp2(b)).astype(BF16), state_t.astype(BF16),
                             nt_dims, preferred_element_type=F32)
             + jnp.dot(a.astype(BF16), v, preferred_element_type=F32))
        b_last = b[c - 1:c, :]
        kd = (kk * jnp.exp2(b_last - b)).astype(BF16)
        upd_t = lax.dot_general(v, kd, (((0,), (0,)), ((), ())),
                                preferred_element_type=F32)
        state_ref[hh] = jnp.exp2(b_last) * state_t + upd_t

        o = o * lax.rsqrt(jnp.mean(o * o, axis=-1, keepdims=True) + EPS) * gw_ref[...]
        o_ref[rows, lanes] = (o * g_ref[rows, lanes]).astype(o_ref.dtype)

    def chunk(ci, carry):
        rows = pl.ds(pl.multiple_of(ci * c, c), c)
        for hh in range(REC_HEADS):
            one_head(hh, rows)
        return carry

    lax.fori_loop(0, REC_ROWS // c, chunk, 0, unroll=2)


def _recurrence(proj, gnorm_w):
    s = proj.shape[0]
    blk = (REC_ROWS, REC_HEADS * HEAD_DIM)
    groups = N_HEADS // REC_HEADS
    seg = lambda k: (lambda h, i: (i, k * groups + h))
    return pl.pallas_call(
        _rec_kernel,
        out_shape=jax.ShapeDtypeStruct((s, WIDTH), BF16),
        grid=(groups, s // REC_ROWS),
        in_specs=[pl.BlockSpec(blk, seg(SEG_HQ)),
                  pl.BlockSpec(blk, seg(SEG_HF)),
                  pl.BlockSpec(blk, seg(SEG_HI)),
                  pl.BlockSpec(blk, seg(SEG_HG)),
                  pl.BlockSpec((1, HEAD_DIM), lambda h, i: (0, 0))],
        out_specs=pl.BlockSpec(blk, lambda h, i: (i, h)),
        scratch_shapes=[pltpu.VMEM((REC_HEADS, HEAD_DIM, HEAD_DIM), F32)],
        compiler_params=_params("parallel", "arbitrary"),
        name="hgrn",
    )(proj, proj, proj, proj, gnorm_w)


def _out_kernel(x_ref, ya_ref, yr_ref, w_ref, g_ref, nw_ref, sh_ref, sc_ref,
                x1_ref, h2_ref):
    y = (jnp.dot(ya_ref[...], w_ref[0:WIDTH, :], preferred_element_type=F32)
         + jnp.dot(yr_ref[...], w_ref[WIDTH:, :], preferred_element_type=F32))
    x1 = x_ref[...] + g_ref[...] * y
    x1_ref[...] = x1
    h2_ref[...] = _norm_modulate(x1, nw_ref[...], sc_ref[...], sh_ref[...]).astype(BF16)


def _out_proj(x2d, y_attn, y_rec, w_out_bf16, mod, norm2_w, tm=256):
    s, d = x2d.shape
    vec = lambda k: pl.BlockSpec((1, d), lambda i: (0, k))
    return pl.pallas_call(
        _out_kernel,
        out_shape=(jax.ShapeDtypeStruct((s, d), F32),
                   jax.ShapeDtypeStruct((s, d), BF16)),
        grid=(s // tm,),
        in_specs=[pl.BlockSpec((tm, d), lambda i: (i, 0)),
                  pl.BlockSpec((tm, WIDTH), lambda i: (i, 0)),
                  pl.BlockSpec((tm, WIDTH), lambda i: (i, 0)),
                  pl.BlockSpec((2 * WIDTH, d), lambda i: (0, 0)),
                  vec(2),
                  pl.BlockSpec((1, d), lambda i: (0, 0)),
                  vec(3), vec(4)],
        out_specs=(pl.BlockSpec((tm, d), lambda i: (i, 0)),
                   pl.BlockSpec((tm, d), lambda i: (i, 0))),
        compiler_params=_params("parallel"),
        name="out_proj",
    )(x2d, y_attn, y_rec, w_out_bf16, mod, norm2_w, mod, mod)


def _conv_rows(stage, half, cols, cw_ref, cb_ref):
    w0, w1, w2 = cw_ref[0:1, cols], cw_ref[1:2, cols], cw_ref[2:3, cols]
    b = cb_ref[:, cols]
    r = [stage[pl.ds(SUBLANES - 2 + k, half, stride=2), :] for k in range(4)]
    even = r[2] * w2 + r[1] * w1 + r[0] * w0 + b
    odd = r[3] * w2 + r[2] * w1 + r[1] * w0 + b
    return even, odd


def _up_kernel(h_ref, wg_ref, wu_ref, cwg_ref, cwu_ref, cbg_ref, cbu_ref, o_ref,
               tailg_ref, tailu_ref, stage_ref, act_ref):
    @pl.when(pl.program_id(1) == 0)
    def _():
        tailg_ref[...] = jnp.zeros_like(tailg_ref)
        tailu_ref[...] = jnp.zeros_like(tailu_ref)

    tm, n = o_ref.shape
    half = tm // 2
    for c in range(n // SUB_N):
        cols = slice(c * SUB_N, (c + 1) * SUB_N)
        g = jnp.dot(h_ref[...], wg_ref[:, cols], preferred_element_type=F32)
        u = jnp.dot(h_ref[...], wu_ref[:, cols], preferred_element_type=F32)
        for lh in range(SUB_N // LANES):
            lanes = slice(c * SUB_N + lh * LANES, c * SUB_N + (lh + 1) * LANES)
            sg = stage_ref.at[c % 2, 0, lh]
            su = stage_ref.at[c % 2, 1, lh]
            sg[0:SUBLANES, :] = tailg_ref[:, lanes]
            su[0:SUBLANES, :] = tailu_ref[:, lanes]
            sg[SUBLANES:, :] = g[:, lh * LANES:(lh + 1) * LANES]
            su[SUBLANES:, :] = u[:, lh * LANES:(lh + 1) * LANES]
            tailg_ref[:, lanes] = sg[tm:, :]
            tailu_ref[:, lanes] = su[tm:, :]
            plane = act_ref.at[lh]
            gates = _conv_rows(sg, half, lanes, cwg_ref, cbg_ref)
            ups = _conv_rows(su, half, lanes, cwu_ref, cbu_ref)
            for parity in range(2):
                plane[pl.ds(parity, half, stride=2), :] = _silu(gates[parity]) * ups[parity]
            o_ref[:, lanes] = plane[...].astype(o_ref.dtype)


def _ffn_up(h2, w_up_bf16, conv_w, conv_b, tm=1024, tn=D_FF // 2):
    s, d = h2.shape
    nj = D_FF // tn
    once = pl.Buffered(1)
    return pl.pallas_call(
        _up_kernel,
        out_shape=jax.ShapeDtypeStruct((s, D_FF), BF16),
        grid=(nj, s // tm),
        in_specs=[pl.BlockSpec((tm, d), lambda j, i: (i, 0)),
                  pl.BlockSpec((d, tn), lambda j, i: (0, j), pipeline_mode=once),
                  pl.BlockSpec((d, tn), lambda j, i: (0, nj + j), pipeline_mode=once),
                  pl.BlockSpec((CONV_WIDTH, tn), lambda j, i: (0, j)),
                  pl.BlockSpec((CONV_WIDTH, tn), lambda j, i: (0, nj + j)),
                  pl.BlockSpec((1, tn), lambda j, i: (0, j)),
                  pl.BlockSpec((1, tn), lambda j, i: (0, nj + j))],
        out_specs=pl.BlockSpec((tm, tn), lambda j, i: (i, j)),
        scratch_shapes=[pltpu.VMEM((SUBLANES, tn), F32),
                        pltpu.VMEM((SUBLANES, tn), F32),
                        pltpu.VMEM((2, 2, SUB_N // LANES, SUBLANES + tm, LANES), F32),
                        pltpu.VMEM((SUB_N // LANES, tm, LANES), F32)],
        compiler_params=_params("parallel", "arbitrary"),
        name="ffn_up",
    )(h2, w_up_bf16, w_up_bf16, conv_w, conv_w, conv_b, conv_b)


def _down_kernel(a_ref, w_ref, x1_ref, g_ref, fw_ref, o_ref):
    y = jnp.dot(a_ref[...], w_ref[...], preferred_element_type=F32)
    x2 = x1_ref[...] + g_ref[...] * y
    ms = jnp.mean(x2 * x2, axis=-1, keepdims=True)
    o_ref[...] = (x2 * lax.rsqrt(ms + EPS)) * fw_ref[...]


def _ffn_down(act, w_down_bf16, x1, mod, final_w, tm=256):
    s, d = x1.shape
    return pl.pallas_call(
        _down_kernel,
        out_shape=jax.ShapeDtypeStruct((s, d), F32),
        grid=(s // tm,),
        in_specs=[pl.BlockSpec((tm, D_FF), lambda i: (i, 0)),
                  pl.BlockSpec((D_FF, d), lambda i: (0, 0), pipeline_mode=pl.Buffered(1)),
                  pl.BlockSpec((tm, d), lambda i: (i, 0)),
                  pl.BlockSpec((1, d), lambda i: (0, 5)),
                  pl.BlockSpec((1, d), lambda i: (0, 0))],
        out_specs=pl.BlockSpec((tm, d), lambda i: (i, 0)),
        compiler_params=_params("parallel"),
        name="ffn_down",
    )(act, w_down_bf16, x1, mod, final_w)


def _rotary_inv_freq():
    inv = 1.0 / (ROPE_THETA ** (jnp.arange(0, ROT_DIM, 2, dtype=F32) / ROT_DIM))
    return jnp.tile(jnp.concatenate([inv, inv]), LANES // ROT_DIM).reshape(1, LANES)


def kernel(x, c, positions, w_ada, b_ada, norm1_w, w_in, lb_logits, gnorm_w, w_out,
           norm2_w, w_up, conv_w, conv_b, w_down, final_norm_w):
    b, s, d = x.shape
    assert b == 1 and d == D_MODEL and w_ada.shape[0] == 1
    x2d = x.reshape(s, d)
    pos_col = positions.reshape(s, 1)

    mod = _modulation(c.reshape(d, 1), w_ada[0], b_ada)
    proj = _in_proj(x2d, pos_col, _rotary_inv_freq(), norm1_w, mod, mod, lb_logits,
                    _arrange_w_in(w_in[0]))
    y_attn = _attention(proj)
    y_rec = _recurrence(proj, gnorm_w)
    x1, h2 = _out_proj(x2d, y_attn, y_rec, w_out[0].astype(BF16), mod, norm2_w)
    act = _ffn_up(h2, w_up[0].astype(BF16), conv_w[0], conv_b)
    out = _ffn_down(act, w_down[0].astype(BF16), x1, mod, final_norm_w.reshape(1, d))
    return out.reshape(b, s, d)
```

```python
import jax
import jax.numpy as jnp
from jax import lax
from jax.experimental import pallas as pl
from jax.experimental.pallas import tpu as pltpu

F32 = jnp.float32
BF16 = jnp.bfloat16

D_MODEL = 2048
N_HEADS = 8
HEAD_DIM = 128
WIDTH = N_HEADS * HEAD_DIM
ROT_DIM = HEAD_DIM // 4
ROPE_THETA = 500000.0
DILATIONS = (1, 4, 16)
WINDOW_STEPS = 128
D_FF = 5632
CONV_WIDTH = 3
EPS = 1e-6
MASKED = -1e30
LOG2_E = 1.4426950408889634

VMEM_LIMIT = 56 * 1024 * 1024
SUB_N = 256
LANES = 128
SUBLANES = 8


def _sigmoid(x):
    return 1.0 / (1.0 + jnp.exp(-x))


def _silu(x):
    return x * _sigmoid(x)


def _params(*semantics):
    return pltpu.CompilerParams(dimension_semantics=semantics,
                                vmem_limit_bytes=VMEM_LIMIT)


def _mod_kernel(c_ref, w_ref, b_ref, o_ref):
    ca = _silu(c_ref[...])
    o_ref[...] = jnp.sum(w_ref[...] * ca, axis=0, keepdims=True) + b_ref[...]


def _modulation(c_col, w_ada, b_ada):
    d, n = w_ada.shape
    tn = 512
    return pl.pallas_call(
        _mod_kernel,
        out_shape=jax.ShapeDtypeStruct((1, n), F32),
        grid=(n // tn,),
        in_specs=[pl.BlockSpec((d, 1), lambda j: (0, 0)),
                  pl.BlockSpec((d, tn), lambda j: (0, j)),
                  pl.BlockSpec((1, tn), lambda j: (0, j))],
        out_specs=pl.BlockSpec((1, tn), lambda j: (0, j)),
        compiler_params=_params("parallel"),
        name="mod",
    )(c_col, w_ada, b_ada)


def _norm_modulate(x, norm_w, scale, shift):
    ms = jnp.mean(x * x, axis=-1, keepdims=True)
    return (x * lax.rsqrt(ms + EPS)) * norm_w * (1.0 + scale) + shift


SEG_AQ, SEG_AV, SEG_AK, SEG_HI, SEG_HF, SEG_HQ, SEG_HG = range(7)
SEG_SOURCE = (0, 2, 1, 5, 4, 3, 6)


def _in_kernel(x_ref, pos_ref, invf_ref, nw_ref, sh_ref, sc_ref, lbl_ref, w_ref, o_ref,
               h_ref, cos_ref, sina_ref, sinb_ref):
    h = _norm_modulate(x_ref[...], nw_ref[...], sc_ref[...], sh_ref[...])
    h_ref[...] = h.astype(BF16)

    tm = pos_ref.shape[0]
    groups = LANES // ROT_DIM
    rows = tm // groups
    lane = lax.broadcasted_iota(jnp.int32, (rows, LANES), 1)
    posf = pos_ref[...].astype(F32)
    packed = jnp.broadcast_to(posf[(groups - 1) * rows:], (rows, LANES))
    for g in range(groups - 2, -1, -1):
        packed = jnp.where(lane < (g + 1) * ROT_DIM,
                           jnp.broadcast_to(posf[g * rows:(g + 1) * rows], (rows, LANES)),
                           packed)
    ang = packed * invf_ref[...]
    c = jnp.cos(ang)
    s = jnp.sin(ang)
    half = ROT_DIM // 2
    for g in range(groups):
        cg = pltpu.roll(c, (LANES - g * ROT_DIM) % LANES, 1) if g else c
        sg = pltpu.roll(s, (LANES - g * ROT_DIM) % LANES, 1) if g else s
        dst = slice(g * rows, (g + 1) * rows)
        cos_ref[dst, :] = jnp.where(lane < ROT_DIM, cg, 1.0)
        sina_ref[dst, :] = jnp.where(lane < half, -sg, 0.0)
        sinb_ref[dst, :] = jnp.where((lane >= half) & (lane < ROT_DIM), sg, 0.0)

    def segment(seg, epilogue):
        for c0 in range(0, WIDTH, SUB_N):
            cols = slice(c0, c0 + SUB_N)
            out_cols = slice(seg * WIDTH + c0, seg * WIDTH + c0 + SUB_N)
            r = jnp.dot(h_ref[...], w_ref[:, out_cols], preferred_element_type=F32)
            o_ref[:, out_cols] = epilogue(r, cols)

    def rotary(scale):
        def epilogue(r, cols):
            outs = []
            for hh in range(SUB_N // HEAD_DIM):
                rh = r[:, hh * HEAD_DIM:(hh + 1) * HEAD_DIM]
                out = (rh * cos_ref[...]
                       + pltpu.roll(rh, HEAD_DIM - half, 1) * sina_ref[...]
                       + pltpu.roll(rh, half, 1) * sinb_ref[...])
                outs.append(out * scale if scale != 1.0 else out)
            return jnp.concatenate(outs, axis=1)
        return epilogue

    def plain(r, cols):
        return r

    lg = lbl_ref[...]
    e = jnp.exp(lg - jnp.max(lg, axis=0, keepdims=True))
    lb = e[0:1, :] / jnp.sum(e, axis=0, keepdims=True)

    def log2_forget(r, cols):
        f = lb[:, cols] + (1.0 - lb[:, cols]) * _sigmoid(r)
        return jnp.log2(f)

    segment(SEG_AQ, rotary(HEAD_DIM ** -0.5 * LOG2_E))
    segment(SEG_AV, plain)
    segment(SEG_AK, rotary(1.0))
    segment(SEG_HI, plain)
    segment(SEG_HF, log2_forget)
    segment(SEG_HQ, lambda r, cols: _silu(r) * (HEAD_DIM ** -0.5))
    segment(SEG_HG, lambda r, cols: _silu(r))


def _arrange_w_in(w_in):
    segs = [w_in[:, k * WIDTH:(k + 1) * WIDTH] for k in SEG_SOURCE]
    return jnp.concatenate(segs, axis=1).astype(BF16)


def _in_proj(x2d, pos_col, invf, norm_w, shift, scale, lb_logits, w_in_bf16, tm=256):
    s, d = x2d.shape
    n = w_in_bf16.shape[1]
    return pl.pallas_call(
        _in_kernel,
        out_shape=jax.ShapeDtypeStruct((s, n), F32),
        grid=(s // tm,),
        in_specs=[pl.BlockSpec((tm, d), lambda i: (i, 0)),
                  pl.BlockSpec((tm, 1), lambda i: (i, 0)),
                  pl.BlockSpec((1, HEAD_DIM), lambda i: (0, 0)),
                  pl.BlockSpec((1, d), lambda i: (0, 0)),
                  pl.BlockSpec((1, d), lambda i: (0, 0)),
                  pl.BlockSpec((1, d), lambda i: (0, 1)),
                  pl.BlockSpec(lb_logits.shape, lambda i: (0, 0)),
                  pl.BlockSpec((d, n), lambda i: (0, 0), pipeline_mode=pl.Buffered(1))],
        out_specs=pl.BlockSpec((tm, n), lambda i: (i, 0)),
        scratch_shapes=[pltpu.VMEM((tm, d), BF16),
                        pltpu.VMEM((tm, HEAD_DIM), F32),
                        pltpu.VMEM((tm, HEAD_DIM), F32),
                        pltpu.VMEM((tm, HEAD_DIM), F32)],
        compiler_params=_params("parallel"),
        name="in_proj",
    )(x2d, pos_col, invf, norm_w, shift, scale, lb_logits, w_in_bf16)


ATT_TILE = 2048
ATT_BLK = 128
ATT_P4 = 4


def _attn_kernel(q_ref, k_ref, v_ref, o_ref,
                 q4_ref, k4_ref, v4_ref, k16_ref, v16_ref, ktail_ref, vtail_ref,
                 acc_ref, m_ref, l_ref, y_ref):
    t = pl.program_id(1)
    cur = t % 2
    prev = 1 - cur
    span4 = ATT_TILE // ATT_P4

    @pl.when(t == 0)
    def _():
        k4_ref[1] = jnp.zeros_like(k4_ref[1])
        v4_ref[1] = jnp.zeros_like(v4_ref[1])
        k16_ref[1] = jnp.zeros_like(k16_ref[1])
        v16_ref[1] = jnp.zeros_like(v16_ref[1])
        ktail_ref[...] = jnp.zeros_like(ktail_ref)
        vtail_ref[...] = jnp.zeros_like(vtail_ref)

    qi = lax.broadcasted_iota(jnp.int32, (ATT_BLK, 2 * ATT_BLK), 0)
    ki = lax.broadcasted_iota(jnp.int32, (ATT_BLK, 2 * ATT_BLK), 1)
    band = (ki >= qi) & (ki <= qi + WINDOW_STEPS)
    band_first = band & ((ki >= ATT_BLK) | (t > 0))
    nt_dims = (((1,), (1,)), ((), ()))
    ones = jnp.ones((2 * ATT_BLK, HEAD_DIM), BF16)

    def block(qb, kk, vv, valid):
        s = lax.dot_general(qb, kk, nt_dims, preferred_element_type=F32)
        s = jnp.where(valid, s, MASKED)
        m = jnp.max(s, axis=1, keepdims=True)
        pe = jnp.exp2(s - m).astype(BF16)
        ext = jnp.dot(pe, jnp.concatenate([vv, ones], axis=1), preferred_element_type=F32)
        return (ext[:, :HEAD_DIM], jnp.broadcast_to(m, (ATT_BLK, HEAD_DIM)),
                ext[:, HEAD_DIM:])

    def put(p, rows, res):
        acc_ref[p, rows, :], m_ref[p, rows, :], l_ref[p, rows, :] = res

    for r in range(ATT_P4):
        src = pl.ds(r, span4, stride=ATT_P4)
        dst = pl.ds(r * span4, span4)
        q4_ref[dst, :] = q_ref[src, :]
        k4_ref[cur, dst, :] = k_ref[src, :]
        v4_ref[cur, dst, :] = v_ref[src, :]

    for b in range(ATT_TILE // ATT_BLK):
        rows = pl.ds(b * ATT_BLK, ATT_BLK)
        qb = q_ref[rows, :].astype(BF16)
        if b > 0:
            both = pl.ds((b - 1) * ATT_BLK, 2 * ATT_BLK)
            kk = k_ref[both, :].astype(BF16)
            vv = v_ref[both, :].astype(BF16)
            valid = band
        else:
            kk = jnp.concatenate([ktail_ref[...], k_ref[rows, :].astype(BF16)], axis=0)
            vv = jnp.concatenate([vtail_ref[...], v_ref[rows, :].astype(BF16)], axis=0)
            valid = band_first
        put(0, rows, block(qb, kk, vv, valid))
    last = pl.ds(ATT_TILE - ATT_BLK, ATT_BLK)
    ktail_ref[...] = k_ref[last, :].astype(BF16)
    vtail_ref[...] = v_ref[last, :].astype(BF16)

    for r in range(ATT_P4):
        for b in range(span4 // ATT_BLK):
            base = r * span4 + b * ATT_BLK
            rows = pl.ds(base, ATT_BLK)
            qb = q4_ref[rows, :].astype(BF16)
            if b > 0:
                both = pl.ds(base - ATT_BLK, 2 * ATT_BLK)
                kk = k4_ref[cur, both, :].astype(BF16)
                vv = v4_ref[cur, both, :].astype(BF16)
                valid = band
            else:
                back = pl.ds((r + 1) * span4 - ATT_BLK, ATT_BLK)
                kk = jnp.concatenate([k4_ref[prev, back, :].astype(BF16),
                                      k4_ref[cur, rows, :].astype(BF16)], axis=0)
                vv = jnp.concatenate([v4_ref[prev, back, :].astype(BF16),
                                      v4_ref[cur, rows, :].astype(BF16)], axis=0)
                valid = band_first
            put(1, rows, block(qb, kk, vv, valid))

    for r in range(ATT_P4):
        for r2 in range(ATT_P4):
            src = pl.ds(r * span4 + r2, ATT_BLK, stride=ATT_P4)
            dst = pl.ds((ATT_P4 * r2 + r) * ATT_BLK, ATT_BLK)
            qb = q4_ref[src, :].astype(BF16)
            kc = k4_ref[cur, src, :].astype(BF16)
            vc = v4_ref[cur, src, :].astype(BF16)
            k16_ref[cur, dst, :] = kc
            v16_ref[cur, dst, :] = vc
            kk = jnp.concatenate([k16_ref[prev, dst, :], kc], axis=0)
            vv = jnp.concatenate([v16_ref[prev, dst, :], vc], axis=0)
            put(2, src, block(qb, kk, vv, band_first))

    for r in range(ATT_P4):
        for b in range(span4 // ATT_BLK):
            rows4 = pl.ds(r * span4 + b * ATT_BLK, ATT_BLK)
            nat = pl.ds(r + ATT_P4 * b * ATT_BLK, ATT_BLK, stride=ATT_P4)
            ms = [m_ref[0, nat, :], m_ref[1, rows4, :], m_ref[2, rows4, :]]
            ls = [l_ref[0, nat, :], l_ref[1, rows4, :], l_ref[2, rows4, :]]
            accs = [acc_ref[0, nat, :], acc_ref[1, rows4, :], acc_ref[2, rows4, :]]
            m_all = jnp.maximum(jnp.maximum(ms[0], ms[1]), ms[2])
            num = jnp.zeros((ATT_BLK, HEAD_DIM), F32)
            den = jnp.zeros((ATT_BLK, HEAD_DIM), F32)
            for p in range(len(DILATIONS)):
                w = jnp.exp2(ms[p] - m_all)
                num = num + w * accs[p]
                den = den + w * ls[p]
            y_ref[nat, :] = num / den
    o_ref[...] = y_ref[...].astype(o_ref.dtype)


def _attention(proj):
    s = proj.shape[0]
    nt = s // ATT_TILE
    blk = (ATT_TILE, HEAD_DIM)
    n_pat = len(DILATIONS)
    return pl.pallas_call(
        _attn_kernel,
        out_shape=jax.ShapeDtypeStruct((s, WIDTH), BF16),
        grid=(N_HEADS, nt),
        in_specs=[pl.BlockSpec(blk, lambda h, t: (t, SEG_AQ * N_HEADS + h)),
                  pl.BlockSpec(blk, lambda h, t: (t, SEG_AK * N_HEADS + h)),
                  pl.BlockSpec(blk, lambda h, t: (t, SEG_AV * N_HEADS + h))],
        out_specs=pl.BlockSpec(blk, lambda h, t: (t, h)),
        scratch_shapes=[pltpu.VMEM(blk, F32),
                        pltpu.VMEM((2,) + blk, F32),
                        pltpu.VMEM((2,) + blk, F32),
                        pltpu.VMEM((2,) + blk, BF16),
                        pltpu.VMEM((2,) + blk, BF16),
                        pltpu.VMEM((ATT_BLK, HEAD_DIM), BF16),
                        pltpu.VMEM((ATT_BLK, HEAD_DIM), BF16),
                        pltpu.VMEM((n_pat,) + blk, F32),
                        pltpu.VMEM((n_pat,) + blk, F32),
                        pltpu.VMEM((n_pat,) + blk, F32),
                        pltpu.VMEM(blk, F32)],
        compiler_params=_params("parallel", "arbitrary"),
        name="attn",
    )(proj, proj, proj)


REC_CHUNK = 128
REC_LEVELS = 7
REC_ROWS = 1024
REC_HEADS = 4


def _block_scans(lf):
    c = lf.shape[0]
    fwd, rev = [lf], [lf]
    x, y = lf, lf
    for j in range(REC_LEVELS):
        m = 1 << j
        if m < SUBLANES:
            x3 = x.reshape(c // SUBLANES, SUBLANES, HEAD_DIM)
            y3 = y.reshape(c // SUBLANES, SUBLANES, HEAD_DIM)
            sub = lax.broadcasted_iota(jnp.int32, x3.shape, 1)
            block0 = sub & ~(2 * m - 1)
            upper = (sub & m) != 0
            addx = jnp.where(upper, jnp.take_along_axis(x3, block0 + (m - 1), axis=1), 0.0)
            addy = jnp.where(upper, 0.0, jnp.take_along_axis(y3, block0 + m, axis=1))
            x = x + addx.reshape(c, HEAD_DIM)
            y = y + addy.reshape(c, HEAD_DIM)
        else:
            xs, ys = [], []
            for b in range(c // (2 * m)):
                lo = x[b * 2 * m: b * 2 * m + m]
                hi = x[b * 2 * m + m: (b + 1) * 2 * m]
                xs += [lo, hi + lo[m - 1:m, :]]
                lo = y[b * 2 * m: b * 2 * m + m]
                hi = y[b * 2 * m + m: (b + 1) * 2 * m]
                ys += [lo + hi[0:1, :], hi]
            x = jnp.concatenate(xs, axis=0)
            y = jnp.concatenate(ys, axis=0)
        fwd.append(x)
        rev.append(y)
    return fwd, rev


def _rec_kernel(q_ref, lf_ref, v_ref, g_ref, gw_ref, o_ref, state_ref):
    @pl.when(pl.program_id(1) == 0)
    def _():
        state_ref[...] = jnp.zeros_like(state_ref)

    c = REC_CHUNK
    ti = lax.broadcasted_iota(jnp.int32, (c, c), 0)
    si = lax.broadcasted_iota(jnp.int32, (c, c), 1)
    diff = ti ^ si
    level = jnp.full((c, c), -1, jnp.int32)
    for j in range(REC_LEVELS):
        level = jnp.where((diff >> j) == 1, j, level)
    level = jnp.where(si > ti, -1, jnp.where(si == ti, REC_LEVELS, level))
    nt_dims = (((1,), (1,)), ((), ()))

    def one_head(hh, rows):
        lanes = slice(hh * HEAD_DIM, (hh + 1) * HEAD_DIM)
        qf = q_ref[rows, lanes]
        lf = lf_ref[rows, lanes]
        v = v_ref[rows, lanes].astype(BF16)
        kk = 1.0 - jnp.exp2(lf)
        fwd, rev = _block_scans(lf)
        b = fwd[REC_LEVELS]

        a = jnp.where(level == REC_LEVELS,
                      lax.dot_general(qf.astype(BF16), kk.astype(BF16), nt_dims,
                                      preferred_element_type=F32), 0.0)
        for j in range(REC_LEVELS):
            m = 1 << j
            if m < SUBLANES:
                ql = (qf * jnp.exp2(fwd[j])).astype(BF16)
                kl = (kk * jnp.exp2(rev[j] - lf)).astype(BF16)
                al = lax.dot_general(ql, kl, nt_dims, preferred_element_type=F32)
                a = jnp.where(level == j, al, a)
                continue
            lows = [slice(b, b + m) for b in range(0, c, 2 * m)]
            ups = [slice(b + m, b + 2 * m) for b in range(0, c, 2 * m)]
            pick = lambda x, parts: jnp.concatenate([x[p] for p in parts], axis=0)
            ql = (pick(qf, ups) * jnp.exp2(pick(fwd[j], ups))).astype(BF16)
            k_low = pick(kk, lows) * jnp.exp2(pick(rev[j], lows) - pick(lf, lows))
            zeros = jnp.zeros((m, HEAD_DIM), F32)
            kl = jnp.concatenate(
                [piece for b in range(len(lows))
                 for piece in (k_low[b * m:(b + 1) * m], zeros)], axis=0).astype(BF16)
            al = lax.dot_general(ql, kl, nt_dims, preferred_element_type=F32)
            a_up = jnp.where(pick(level, ups) == j, al, pick(a, ups))
            a = jnp.concatenate(
                [piece for b in range(len(lows))
                 for piece in (a[lows[b]], a_up[b * m:(b + 1) * m])], axis=0)

        state_t = state_ref[hh]
        o = (lax.dot_general((qf * jnp.exp2(b)).astype(BF16), state_t.astype(BF16),
                             nt_dims, preferred_element_type=F32)
             + jnp.dot(a.astype(BF16), v, preferred_element_type=F32))
        b_last = b[c - 1:c, :]
        kd = (kk * jnp.exp2(b_last - b)).astype(BF16)
        upd_t = lax.dot_general(v, kd, (((0,), (0,)), ((), ())),
                                preferred_element_type=F32)
        state_ref[hh] = jnp.exp2(b_last) * state_t + upd_t

        o = o * lax.rsqrt(jnp.mean(o * o, axis=-1, keepdims=True) + EPS) * gw_ref[...]
        o_ref[rows, lanes] = (o * g_ref[rows, lanes]).astype(o_ref.dtype)

    def chunk(ci, carry):
        rows = pl.ds(pl.multiple_of(ci * c, c), c)
        for hh in range(REC_HEADS):
            one_head(hh, rows)
        return carry

    lax.fori_loop(0, REC_ROWS // c, chunk, 0, unroll=2)


def _recurrence(proj, gnorm_w):
    s = proj.shape[0]
    blk = (REC_ROWS, REC_HEADS * HEAD_DIM)
    groups = N_HEADS // REC_HEADS
    seg = lambda k: (lambda h, i: (i, k * groups + h))
    return pl.pallas_call(
        _rec_kernel,
        out_shape=jax.ShapeDtypeStruct((s, WIDTH), BF16),
        grid=(groups, s // REC_ROWS),
        in_specs=[pl.BlockSpec(blk, seg(SEG_HQ)),
                  pl.BlockSpec(blk, seg(SEG_HF)),
                  pl.BlockSpec(blk, seg(SEG_HI)),
                  pl.BlockSpec(blk, seg(SEG_HG)),
                  pl.BlockSpec((1, HEAD_DIM), lambda h, i: (0, 0))],
        out_specs=pl.BlockSpec(blk, lambda h, i: (i, h)),
        scratch_shapes=[pltpu.VMEM((REC_HEADS, HEAD_DIM, HEAD_DIM), F32)],
        compiler_params=_params("parallel", "arbitrary"),
        name="hgrn",
    )(proj, proj, proj, proj, gnorm_w)


def _out_kernel(x_ref, ya_ref, yr_ref, w_ref, g_ref, nw_ref, sh_ref, sc_ref,
                x1_ref, h2_ref):
    y = (jnp.dot(ya_ref[...], w_ref[0:WIDTH, :], preferred_element_type=F32)
         + jnp.dot(yr_ref[...], w_ref[WIDTH:, :], preferred_element_type=F32))
    x1 = x_ref[...] + g_ref[...] * y
    x1_ref[...] = x1
    h2_ref[...] = _norm_modulate(x1, nw_ref[...], sc_ref[...], sh_ref[...]).astype(BF16)


def _out_proj(x2d, y_attn, y_rec, w_out_bf16, mod, norm2_w, tm=512):
    s, d = x2d.shape
    vec = lambda k: pl.BlockSpec((1, d), lambda i: (0, k))
    return pl.pallas_call(
        _out_kernel,
        out_shape=(jax.ShapeDtypeStruct((s, d), F32),
                   jax.ShapeDtypeStruct((s, d), BF16)),
        grid=(s // tm,),
        in_specs=[pl.BlockSpec((tm, d), lambda i: (i, 0)),
                  pl.BlockSpec((tm, WIDTH), lambda i: (i, 0)),
                  pl.BlockSpec((tm, WIDTH), lambda i: (i, 0)),
                  pl.BlockSpec((2 * WIDTH, d), lambda i: (0, 0)),
                  vec(2),
                  pl.BlockSpec((1, d), lambda i: (0, 0)),
                  vec(3), vec(4)],
        out_specs=(pl.BlockSpec((tm, d), lambda i: (i, 0)),
                   pl.BlockSpec((tm, d), lambda i: (i, 0))),
        compiler_params=_params("parallel"),
        name="out_proj",
    )(x2d, y_attn, y_rec, w_out_bf16, mod, norm2_w, mod, mod)


def _conv_rows(stage, half, cols, cw_ref, cb_ref):
    w0, w1, w2 = cw_ref[0:1, cols], cw_ref[1:2, cols], cw_ref[2:3, cols]
    b = cb_ref[:, cols]
    r = [stage[pl.ds(SUBLANES - 2 + k, half, stride=2), :] for k in range(4)]
    even = r[2] * w2 + r[1] * w1 + r[0] * w0 + b
    odd = r[3] * w2 + r[2] * w1 + r[1] * w0 + b
    return even, odd


def _up_kernel(h_ref, wg_ref, wu_ref, cwg_ref, cwu_ref, cbg_ref, cbu_ref, o_ref,
               tailg_ref, tailu_ref, stage_ref, act_ref):
    @pl.when(pl.program_id(1) == 0)
    def _():
        tailg_ref[...] = jnp.zeros_like(tailg_ref)
        tailu_ref[...] = jnp.zeros_like(tailu_ref)

    tm, n = o_ref.shape
    half = tm // 2
    for c in range(n // SUB_N):
        cols = slice(c * SUB_N, (c + 1) * SUB_N)
        g = jnp.dot(h_ref[...], wg_ref[:, cols], preferred_element_type=F32)
        u = jnp.dot(h_ref[...], wu_ref[:, cols], preferred_element_type=F32)
        for lh in range(SUB_N // LANES):
            lanes = slice(c * SUB_N + lh * LANES, c * SUB_N + (lh + 1) * LANES)
            sg = stage_ref.at[c % 2, 0, lh]
            su = stage_ref.at[c % 2, 1, lh]
            sg[0:SUBLANES, :] = tailg_ref[:, lanes]
            su[0:SUBLANES, :] = tailu_ref[:, lanes]
            sg[SUBLANES:, :] = g[:, lh * LANES:(lh + 1) * LANES]
            su[SUBLANES:, :] = u[:, lh * LANES:(lh + 1) * LANES]
            tailg_ref[:, lanes] = sg[tm:, :]
            tailu_ref[:, lanes] = su[tm:, :]
            plane = act_ref.at[lh]
            gates = _conv_rows(sg, half, lanes, cwg_ref, cbg_ref)
            ups = _conv_rows(su, half, lanes, cwu_ref, cbu_ref)
            for parity in range(2):
                plane[pl.ds(parity, half, stride=2), :] = _silu(gates[parity]) * ups[parity]
            o_ref[:, lanes] = plane[...].astype(o_ref.dtype)


def _ffn_up(h2, w_up_bf16, conv_w, conv_b, tm=1024, tn=D_FF // 2):
    s, d = h2.shape
    nj = D_FF // tn
    once = pl.Buffered(1)
    return pl.pallas_call(
        _up_kernel,
        out_shape=jax.ShapeDtypeStruct((s, D_FF), BF16),
        grid=(nj, s // tm),
        in_specs=[pl.BlockSpec((tm, d), lambda j, i: (i, 0)),
                  pl.BlockSpec((d, tn), lambda j, i: (0, j), pipeline_mode=once),
                  pl.BlockSpec((d, tn), lambda j, i: (0, nj + j), pipeline_mode=once),
                  pl.BlockSpec((CONV_WIDTH, tn), lambda j, i: (0, j)),
                  pl.BlockSpec((CONV_WIDTH, tn), lambda j, i: (0, nj + j)),
                  pl.BlockSpec((1, tn), lambda j, i: (0, j)),
                  pl.BlockSpec((1, tn), lambda j, i: (0, nj + j))],
        out_specs=pl.BlockSpec((tm, tn), lambda j, i: (i, j)),
        scratch_shapes=[pltpu.VMEM((SUBLANES, tn), F32),
                        pltpu.VMEM((SUBLANES, tn), F32),
                        pltpu.VMEM((2, 2, SUB_N // LANES, SUBLANES + tm, LANES), F32),
                        pltpu.VMEM((SUB_N // LANES, tm, LANES), F32)],
        compiler_params=_params("parallel", "arbitrary"),
        name="ffn_up",
    )(h2, w_up_bf16, w_up_bf16, conv_w, conv_w, conv_b, conv_b)


def _down_kernel(a_ref, w_ref, x1_ref, g_ref, fw_ref, o_ref):
    y = jnp.dot(a_ref[...], w_ref[...], preferred_element_type=F32)
    x2 = x1_ref[...] + g_ref[...] * y
    ms = jnp.mean(x2 * x2, axis=-1, keepdims=True)
    o_ref[...] = (x2 * lax.rsqrt(ms + EPS)) * fw_ref[...]


def _ffn_down(act, w_down_bf16, x1, mod, final_w, tm=512):
    s, d = x1.shape
    return pl.pallas_call(
        _down_kernel,
        out_shape=jax.ShapeDtypeStruct((s, d), F32),
        grid=(s // tm,),
        in_specs=[pl.BlockSpec((tm, D_FF), lambda i: (i, 0)),
                  pl.BlockSpec((D_FF, d), lambda i: (0, 0), pipeline_mode=pl.Buffered(1)),
                  pl.BlockSpec((tm, d), lambda i: (i, 0)),
                  pl.BlockSpec((1, d), lambda i: (0, 5)),
                  pl.BlockSpec((1, d), lambda i: (0, 0))],
        out_specs=pl.BlockSpec((tm, d), lambda i: (i, 0)),
        compiler_params=_params("parallel"),
        name="ffn_down",
    )(act, w_down_bf16, x1, mod, final_w)


def _rotary_inv_freq():
    inv = 1.0 / (ROPE_THETA ** (jnp.arange(0, ROT_DIM, 2, dtype=F32) / ROT_DIM))
    return jnp.tile(jnp.concatenate([inv, inv]), LANES // ROT_DIM).reshape(1, LANES)


def kernel(x, c, positions, w_ada, b_ada, norm1_w, w_in, lb_logits, gnorm_w, w_out,
           norm2_w, w_up, conv_w, conv_b, w_down, final_norm_w):
    b, s, d = x.shape
    assert b == 1 and d == D_MODEL and w_ada.shape[0] == 1
    x2d = x.reshape(s, d)
    pos_col = positions.reshape(s, 1)

    mod = _modulation(c.reshape(d, 1), w_ada[0], b_ada)
    proj = _in_proj(x2d, pos_col, _rotary_inv_freq(), norm1_w, mod, mod, lb_logits,
                    _arrange_w_in(w_in[0]))
    y_attn = _attention(proj)
    y_rec = _recurrence(proj, gnorm_w)
    x1, h2 = _out_proj(x2d, y_attn, y_rec, w_out[0].astype(BF16), mod, norm2_w)
    act = _ffn_up(h2, w_up[0].astype(BF16), conv_w[0], conv_b)
    out = _ffn_down(act, w_down[0].astype(BF16), x1, mod, final_norm_w.reshape(1, d))
    return out.reshape(b, s, d)
```

```python
import jax
import jax.numpy as jnp
from jax import lax
from jax.experimental import pallas as pl
from jax.experimental.pallas import tpu as pltpu

F32 = jnp.float32
BF16 = jnp.bfloat16

D_MODEL = 2048
N_HEADS = 8
HEAD_DIM = 128
WIDTH = N_HEADS * HEAD_DIM
ROT_DIM = HEAD_DIM // 4
ROPE_THETA = 500000.0
DILATIONS = (1, 4, 16)
WINDOW_STEPS = 128
D_FF = 5632
CONV_WIDTH = 3
EPS = 1e-6
MASKED = -1e30
LOG2_E = 1.4426950408889634

VMEM_LIMIT = 56 * 1024 * 1024
SUB_N = 256
LANES = 128
SUBLANES = 8


def _sigmoid(x):
    return 1.0 / (1.0 + jnp.exp(-x))


def _silu(x):
    return x * _sigmoid(x)


def _params(*semantics):
    return pltpu.CompilerParams(dimension_semantics=semantics,
                                vmem_limit_bytes=VMEM_LIMIT)


def _mod_kernel(c_ref, w_ref, b_ref, o_ref):
    ca = _silu(c_ref[...])
    o_ref[...] = jnp.sum(w_ref[...] * ca, axis=0, keepdims=True) + b_ref[...]


def _modulation(c_col, w_ada, b_ada):
    d, n = w_ada.shape
    tn = 512
    return pl.pallas_call(
        _mod_kernel,
        out_shape=jax.ShapeDtypeStruct((1, n), F32),
        grid=(n // tn,),
        in_specs=[pl.BlockSpec((d, 1), lambda j: (0, 0)),
                  pl.BlockSpec((d, tn), lambda j: (0, j)),
                  pl.BlockSpec((1, tn), lambda j: (0, j))],
        out_specs=pl.BlockSpec((1, tn), lambda j: (0, j)),
        compiler_params=_params("parallel"),
        name="mod",
    )(c_col, w_ada, b_ada)


def _norm_modulate(x, norm_w, scale, shift):
    ms = jnp.mean(x * x, axis=-1, keepdims=True)
    return (x * lax.rsqrt(ms + EPS)) * norm_w * (1.0 + scale) + shift


SEG_AQ, SEG_AV, SEG_AK, SEG_HI, SEG_HF, SEG_HQ, SEG_HG = range(7)
SEG_SOURCE = (0, 2, 1, 5, 4, 3, 6)


def _in_kernel(x_ref, pos_ref, invf_ref, nw_ref, sh_ref, sc_ref, lbl_ref, w_ref, o_ref,
               h_ref, cos_ref, sina_ref, sinb_ref):
    h = _norm_modulate(x_ref[...], nw_ref[...], sc_ref[...], sh_ref[...])
    h_ref[...] = h.astype(BF16)

    tm = pos_ref.shape[0]
    groups = LANES // ROT_DIM
    rows = tm // groups
    lane = lax.broadcasted_iota(jnp.int32, (rows, LANES), 1)
    posf = pos_ref[...].astype(F32)
    packed = jnp.broadcast_to(posf[(groups - 1) * rows:], (rows, LANES))
    for g in range(groups - 2, -1, -1):
        packed = jnp.where(lane < (g + 1) * ROT_DIM,
                           jnp.broadcast_to(posf[g * rows:(g + 1) * rows], (rows, LANES)),
                           packed)
    ang = packed * invf_ref[...]
    c = jnp.cos(ang)
    s = jnp.sin(ang)
    half = ROT_DIM // 2
    for g in range(groups):
        cg = pltpu.roll(c, (LANES - g * ROT_DIM) % LANES, 1) if g else c
        sg = pltpu.roll(s, (LANES - g * ROT_DIM) % LANES, 1) if g else s
        dst = slice(g * rows, (g + 1) * rows)
        cos_ref[dst, :] = jnp.where(lane < ROT_DIM, cg, 1.0)
        sina_ref[dst, :] = jnp.where(lane < half, -sg, 0.0)
        sinb_ref[dst, :] = jnp.where((lane >= half) & (lane < ROT_DIM), sg, 0.0)

    def segment(seg, epilogue):
        for c0 in range(0, WIDTH, SUB_N):
            cols = slice(c0, c0 + SUB_N)
            out_cols = slice(seg * WIDTH + c0, seg * WIDTH + c0 + SUB_N)
            r = jnp.dot(h_ref[...], w_ref[:, out_cols], preferred_element_type=F32)
            o_ref[:, out_cols] = epilogue(r, cols)

    def rotary(scale):
        def epilogue(r, cols):
            outs = []
            for hh in range(SUB_N // HEAD_DIM):
                rh = r[:, hh * HEAD_DIM:(hh + 1) * HEAD_DIM]
                out = (rh * cos_ref[...]
                       + pltpu.roll(rh, HEAD_DIM - half, 1) * sina_ref[...]
                       + pltpu.roll(rh, half, 1) * sinb_ref[...])
                outs.append(out * scale if scale != 1.0 else out)
            return jnp.concatenate(outs, axis=1)
        return epilogue

    def plain(r, cols):
        return r

    lg = lbl_ref[...]
    e = jnp.exp(lg - jnp.max(lg, axis=0, keepdims=True))
    lb = e[0:1, :] / jnp.sum(e, axis=0, keepdims=True)

    def log2_forget(r, cols):
        f = lb[:, cols] + (1.0 - lb[:, cols]) * _sigmoid(r)
        return jnp.log2(f)

    segment(SEG_AQ, rotary(HEAD_DIM ** -0.5 * LOG2_E))
    segment(SEG_AV, plain)
    segment(SEG_AK, rotary(1.0))
    segment(SEG_HI, plain)
    segment(SEG_HF, log2_forget)
    segment(SEG_HQ, lambda r, cols: _silu(r) * (HEAD_DIM ** -0.5))
    segment(SEG_HG, lambda r, cols: _silu(r))


def _arrange_w_in(w_in):
    segs = [w_in[:, k * WIDTH:(k + 1) * WIDTH] for k in SEG_SOURCE]
    return jnp.concatenate(segs, axis=1).astype(BF16)


def _in_proj(x2d, pos_col, invf, norm_w, shift, scale, lb_logits, w_in_bf16, tm=256):
    s, d = x2d.shape
    n = w_in_bf16.shape[1]
    return pl.pallas_call(
        _in_kernel,
        out_shape=jax.ShapeDtypeStruct((s, n), F32),
        grid=(s // tm,),
        in_specs=[pl.BlockSpec((tm, d), lambda i: (i, 0)),
                  pl.BlockSpec((tm, 1), lambda i: (i, 0)),
                  pl.BlockSpec((1, HEAD_DIM), lambda i: (0, 0)),
                  pl.BlockSpec((1, d), lambda i: (0, 0)),
                  pl.BlockSpec((1, d), lambda i: (0, 0)),
                  pl.BlockSpec((1, d), lambda i: (0, 1)),
                  pl.BlockSpec(lb_logits.shape, lambda i: (0, 0)),
                  pl.BlockSpec((d, n), lambda i: (0, 0), pipeline_mode=pl.Buffered(1))],
        out_specs=pl.BlockSpec((tm, n), lambda i: (i, 0)),
        scratch_shapes=[pltpu.VMEM((tm, d), BF16),
                        pltpu.VMEM((tm, HEAD_DIM), F32),
                        pltpu.VMEM((tm, HEAD_DIM), F32),
                        pltpu.VMEM((tm, HEAD_DIM), F32)],
        compiler_params=_params("parallel"),
        name="in_proj",
    )(x2d, pos_col, invf, norm_w, shift, scale, lb_logits, w_in_bf16)


ATT_TILE = 2048
ATT_BLK = 128
ATT_P4 = 4


def _attn_kernel(q_ref, k_ref, v_ref, o_ref,
                 q4_ref, k4_ref, v4_ref, k16_ref, v16_ref, ktail_ref, vtail_ref,
                 acc_ref, m_ref, l_ref, y_ref):
    t = pl.program_id(1)
    cur = t % 2
    prev = 1 - cur
    span4 = ATT_TILE // ATT_P4

    @pl.when(t == 0)
    def _():
        k4_ref[1] = jnp.zeros_like(k4_ref[1])
        v4_ref[1] = jnp.zeros_like(v4_ref[1])
        k16_ref[1] = jnp.zeros_like(k16_ref[1])
        v16_ref[1] = jnp.zeros_like(v16_ref[1])
        ktail_ref[...] = jnp.zeros_like(ktail_ref)
        vtail_ref[...] = jnp.zeros_like(vtail_ref)

    qi = lax.broadcasted_iota(jnp.int32, (ATT_BLK, 2 * ATT_BLK), 0)
    ki = lax.broadcasted_iota(jnp.int32, (ATT_BLK, 2 * ATT_BLK), 1)
    band = (ki >= qi) & (ki <= qi + WINDOW_STEPS)
    band_first = band & ((ki >= ATT_BLK) | (t > 0))
    nt_dims = (((1,), (1,)), ((), ()))
    ones = jnp.ones((2 * ATT_BLK, HEAD_DIM), BF16)

    def block(qb, kk, vv, valid):
        s = lax.dot_general(qb, kk, nt_dims, preferred_element_type=F32)
        s = jnp.where(valid, s, MASKED)
        m = jnp.max(s, axis=1, keepdims=True)
        pe = jnp.exp2(s - m).astype(BF16)
        ext = jnp.dot(pe, jnp.concatenate([vv, ones], axis=1), preferred_element_type=F32)
        return (ext[:, :HEAD_DIM], jnp.broadcast_to(m, (ATT_BLK, HEAD_DIM)),
                ext[:, HEAD_DIM:])

    def put(p, rows, res):
        acc_ref[p, rows, :], m_ref[p, rows, :], l_ref[p, rows, :] = res

    for r in range(ATT_P4):
        src = pl.ds(r, span4, stride=ATT_P4)
        dst = pl.ds(r * span4, span4)
        q4_ref[dst, :] = q_ref[src, :]
        k4_ref[cur, dst, :] = k_ref[src, :]
        v4_ref[cur, dst, :] = v_ref[src, :]

    for b in range(ATT_TILE // ATT_BLK):
        rows = pl.ds(b * ATT_BLK, ATT_BLK)
        qb = q_ref[rows, :].astype(BF16)
        if b > 0:
            both = pl.ds((b - 1) * ATT_BLK, 2 * ATT_BLK)
            kk = k_ref[both, :].astype(BF16)
            vv = v_ref[both, :].astype(BF16)
            valid = band
        else:
            kk = jnp.concatenate([ktail_ref[...], k_ref[rows, :].astype(BF16)], axis=0)
            vv = jnp.concatenate([vtail_ref[...], v_ref[rows, :].astype(BF16)], axis=0)
            valid = band_first
        put(0, rows, block(qb, kk, vv, valid))
    last = pl.ds(ATT_TILE - ATT_BLK, ATT_BLK)
    ktail_ref[...] = k_ref[last, :].astype(BF16)
    vtail_ref[...] = v_ref[last, :].astype(BF16)

    for r in range(ATT_P4):
        for b in range(span4 // ATT_BLK):
            base = r * span4 + b * ATT_BLK
            rows = pl.ds(base, ATT_BLK)
            qb = q4_ref[rows, :].astype(BF16)
            if b > 0:
                both = pl.ds(base - ATT_BLK, 2 * ATT_BLK)
                kk = k4_ref[cur, both, :].astype(BF16)
                vv = v4_ref[cur, both, :].astype(BF16)
                valid = band
            else:
                back = pl.ds((r + 1) * span4 - ATT_BLK, ATT_BLK)
                kk = jnp.concatenate([k4_ref[prev, back, :].astype(BF16),
                                      k4_ref[cur, rows, :].astype(BF16)], axis=0)
                vv = jnp.concatenate([v4_ref[prev, back, :].astype(BF16),
                                      v4_ref[cur, rows, :].astype(BF16)], axis=0)
                valid = band_first
            put(1, rows, block(qb, kk, vv, valid))

    for r in range(ATT_P4):
        for r2 in range(ATT_P4):
            src = pl.ds(r * span4 + r2, ATT_BLK, stride=ATT_P4)
            dst = pl.ds((ATT_P4 * r2 + r) * ATT_BLK, ATT_BLK)
            qb = q4_ref[src, :].astype(BF16)
            kc = k4_ref[cur, src, :].astype(BF16)
            vc = v4_ref[cur, src, :].astype(BF16)
            k16_ref[cur, dst, :] = kc
            v16_ref[cur, dst, :] = vc
            kk = jnp.concatenate([k16_ref[prev, dst, :], kc], axis=0)
            vv = jnp.concatenate([v16_ref[prev, dst, :], vc], axis=0)
            put(2, src, block(qb, kk, vv, band_first))

    for r in range(ATT_P4):
        for b in range(span4 // ATT_BLK):
            rows4 = pl.ds(r * span4 + b * ATT_BLK, ATT_BLK)
            nat = pl.ds(r + ATT_P4 * b * ATT_BLK, ATT_BLK, stride=ATT_P4)
            ms = [m_ref[0, nat, :], m_ref[1, rows4, :], m_ref[2, rows4, :]]
            ls = [l_ref[0, nat, :], l_ref[1, rows4, :], l_ref[2, rows4, :]]
            accs = [acc_ref[0, nat, :], acc_ref[1, rows4, :], acc_ref[2, rows4, :]]
            m_all = jnp.maximum(jnp.maximum(ms[0], ms[1]), ms[2])
            num = jnp.zeros((ATT_BLK, HEAD_DIM), F32)
            den = jnp.zeros((ATT_BLK, HEAD_DIM), F32)
            for p in range(len(DILATIONS)):
                w = jnp.exp2(ms[p] - m_all)
                num = num + w * accs[p]
                den = den + w * ls[p]
            y_ref[nat, :] = num / den
    o_ref[...] = y_ref[...].astype(o_ref.dtype)


def _attention(proj):
    s = proj.shape[0]
    nt = s // ATT_TILE
    blk = (ATT_TILE, HEAD_DIM)
    n_pat = len(DILATIONS)
    return pl.pallas_call(
        _attn_kernel,
        out_shape=jax.ShapeDtypeStruct((s, WIDTH), BF16),
        grid=(N_HEADS, nt),
        in_specs=[pl.BlockSpec(blk, lambda h, t: (t, SEG_AQ * N_HEADS + h)),
                  pl.BlockSpec(blk, lambda h, t: (t, SEG_AK * N_HEADS + h)),
                  pl.BlockSpec(blk, lambda h, t: (t, SEG_AV * N_HEADS + h))],
        out_specs=pl.BlockSpec(blk, lambda h, t: (t, h)),
        scratch_shapes=[pltpu.VMEM(blk, F32),
                        pltpu.VMEM((2,) + blk, F32),
                        pltpu.VMEM((2,) + blk, F32),
                        pltpu.VMEM((2,) + blk, BF16),
                        pltpu.VMEM((2,) + blk, BF16),
                        pltpu.VMEM((ATT_BLK, HEAD_DIM), BF16),
                        pltpu.VMEM((ATT_BLK, HEAD_DIM), BF16),
                        pltpu.VMEM((n_pat,) + blk, F32),
                        pltpu.VMEM((n_pat,) + blk, F32),
                        pltpu.VMEM((n_pat,) + blk, F32),
                        pltpu.VMEM(blk, F32)],
        compiler_params=_params("parallel", "arbitrary"),
        name="attn",
    )(proj, proj, proj)


REC_CHUNK = 128
REC_LEVELS = 7
REC_ROWS = 1024
REC_HEADS = 4


def _block_scans(lf):
    c = lf.shape[0]
    fwd, rev = [lf], [lf]
    x, y = lf, lf
    for j in range(REC_LEVELS):
        m = 1 << j
        if m < SUBLANES:
            x3 = x.reshape(c // SUBLANES, SUBLANES, HEAD_DIM)
            y3 = y.reshape(c // SUBLANES, SUBLANES, HEAD_DIM)
            sub = lax.broadcasted_iota(jnp.int32, x3.shape, 1)
            block0 = sub & ~(2 * m - 1)
            upper = (sub & m) != 0
            addx = jnp.where(upper, jnp.take_along_axis(x3, block0 + (m - 1), axis=1), 0.0)
            addy = jnp.where(upper, 0.0, jnp.take_along_axis(y3, block0 + m, axis=1))
            x = x + addx.reshape(c, HEAD_DIM)
            y = y + addy.reshape(c, HEAD_DIM)
        else:
            xs, ys = [], []
            for b in range(c // (2 * m)):
                lo = x[b * 2 * m: b * 2 * m + m]
                hi = x[b * 2 * m + m: (b + 1) * 2 * m]
                xs += [lo, hi + lo[m - 1:m, :]]
                lo = y[b * 2 * m: b * 2 * m + m]
                hi = y[b * 2 * m + m: (b + 1) * 2 * m]
                ys += [lo + hi[0:1, :], hi]
            x = jnp.concatenate(xs, axis=0)
            y = jnp.concatenate(ys, axis=0)
        fwd.append(x)
        rev.append(y)
    return fwd, rev


def _rec_kernel(q_ref, lf_ref, v_ref, g_ref, gw_ref, o_ref, state_ref):
    @pl.when(pl.program_id(1) == 0)
    def _():
        state_ref[...] = jnp.zeros_like(state_ref)

    c = REC_CHUNK
    ti = lax.broadcasted_iota(jnp.int32, (c, c), 0)
    si = lax.broadcasted_iota(jnp.int32, (c, c), 1)
    diff = ti ^ si
    level = jnp.full((c, c), -1, jnp.int32)
    for j in range(REC_LEVELS):
        level = jnp.where((diff >> j) == 1, j, level)
    level = jnp.where(si > ti, -1, jnp.where(si == ti, REC_LEVELS, level))
    nt_dims = (((1,), (1,)), ((), ()))

    def one_head(hh, rows):
        lanes = slice(hh * HEAD_DIM, (hh + 1) * HEAD_DIM)
        qf = q_ref[rows, lanes]
        lf = lf_ref[rows, lanes]
        v = v_ref[rows, lanes].astype(BF16)
        kk = 1.0 - jnp.exp2(lf)
        fwd, rev = _block_scans(lf)
        b = fwd[REC_LEVELS]

        a = jnp.where(level == REC_LEVELS,
                      lax.dot_general(qf.astype(BF16), kk.astype(BF16), nt_dims,
                                      preferred_element_type=F32), 0.0)
        for j in range(REC_LEVELS):
            m = 1 << j
            if m < SUBLANES:
                ql = (qf * jnp.exp2(fwd[j])).astype(BF16)
                kl = (kk * jnp.exp2(rev[j] - lf)).astype(BF16)
                al = lax.dot_general(ql, kl, nt_dims, preferred_element_type=F32)
                a = jnp.where(level == j, al, a)
                continue
            lows = [slice(b, b + m) for b in range(0, c, 2 * m)]
            ups = [slice(b + m, b + 2 * m) for b in range(0, c, 2 * m)]
            pick = lambda x, parts: jnp.concatenate([x[p] for p in parts], axis=0)
            ql = (pick(qf, ups) * jnp.exp2(pick(fwd[j], ups))).astype(BF16)
            k_low = pick(kk, lows) * jnp.exp2(pick(rev[j], lows) - pick(lf, lows))
            zeros = jnp.zeros((m, HEAD_DIM), F32)
            kl = jnp.concatenate(
                [piece for b in range(len(lows))
                 for piece in (k_low[b * m:(b + 1) * m], zeros)], axis=0).astype(BF16)
            al = lax.dot_general(ql, kl, nt_dims, preferred_element_type=F32)
            a_up = jnp.where(pick(level, ups) == j, al, pick(a, ups))
            a = jnp.concatenate(
                [piece for b in range(len(lows))
                 for piece in (a[lows[b]], a_up[b * m:(b + 1) * m])], axis=0)

        state_t = state_ref[hh]
        o = (lax.dot_general((qf * jnp.exp2(b)).astype(BF16), state_t.astype(BF16),
                             nt_dims, preferred_element_type=F32)
             + jnp.dot(a.astype(BF16), v, preferred_element_type=F32))
        b_last = b[c - 1:c, :]
        kd = (kk * jnp.exp2(b_last - b)).astype(BF16)
        upd_t = lax.dot_general(v, kd, (((0,), (0,)), ((), ())),
                                preferred_element_type=F32)
        state_ref[hh] = jnp.exp2(b_last) * state_t + upd_t

        o = o * lax.rsqrt(jnp.mean(o * o, axis=-1, keepdims=True) + EPS) * gw_ref[...]
        o_ref[rows, lanes] = (o * g_ref[rows, lanes]).astype(o_ref.dtype)

    def chunk(ci, carry):
        rows = pl.ds(pl.multiple_of(ci * c, c), c)
        for hh in range(REC_HEADS):
            one_head(hh, rows)
        return carry

    lax.fori_loop(0, REC_ROWS // c, chunk, 0, unroll=2)


def _recurrence(proj, gnorm_w):
    s = proj.shape[0]
    blk = (REC_ROWS, REC_HEADS * HEAD_DIM)
    groups = N_HEADS // REC_HEADS
    seg = lambda k: (lambda h, i: (i, k * groups + h))
    return pl.pallas_call(
        _rec_kernel,
        out_shape=jax.ShapeDtypeStruct((s, WIDTH), BF16),
        grid=(groups, s // REC_ROWS),
        in_specs=[pl.BlockSpec(blk, seg(SEG_HQ)),
                  pl.BlockSpec(blk, seg(SEG_HF)),
                  pl.BlockSpec(blk, seg(SEG_HI)),
                  pl.BlockSpec(blk, seg(SEG_HG)),
                  pl.BlockSpec((1, HEAD_DIM), lambda h, i: (0, 0))],
        out_specs=pl.BlockSpec(blk, lambda h, i: (i, h)),
        scratch_shapes=[pltpu.VMEM((REC_HEADS, HEAD_DIM, HEAD_DIM), F32)],
        compiler_params=_params("parallel", "arbitrary"),
        name="hgrn",
    )(proj, proj, proj, proj, gnorm_w)


def _out_kernel(x_ref, ya_ref, yr_ref, w_ref, g_ref, nw_ref, sh_ref, sc_ref,
                x1_ref, h2_ref):
    y = (jnp.dot(ya_ref[...], w_ref[0:WIDTH, :], preferred_element_type=F32)
         + jnp.dot(yr_ref[...], w_ref[WIDTH:, :], preferred_element_type=F32))
    x1 = x_ref[...] + g_ref[...] * y
    x1_ref[...] = x1
    h2_ref[...] = _norm_modulate(x1, nw_ref[...], sc_ref[...], sh_ref[...]).astype(BF16)


def _out_proj(x2d, y_attn, y_rec, w_out_bf16, mod, norm2_w, tm=512):
    s, d = x2d.shape
    vec = lambda k: pl.BlockSpec((1, d), lambda i: (0, k))
    return pl.pallas_call(
        _out_kernel,
        out_shape=(jax.ShapeDtypeStruct((s, d), F32),
                   jax.ShapeDtypeStruct((s, d), BF16)),
        grid=(s // tm,),
        in_specs=[pl.BlockSpec((tm, d), lambda i: (i, 0)),
                  pl.BlockSpec((tm, WIDTH), lambda i: (i, 0)),
                  pl.BlockSpec((tm, WIDTH), lambda i: (i, 0)),
                  pl.BlockSpec((2 * WIDTH, d), lambda i: (0, 0)),
                  vec(2),
                  pl.BlockSpec((1, d), lambda i: (0, 0)),
                  vec(3), vec(4)],
        out_specs=(pl.BlockSpec((tm, d), lambda i: (i, 0)),
                   pl.BlockSpec((tm, d), lambda i: (i, 0))),
        compiler_params=_params("parallel"),
        name="out_proj",
    )(x2d, y_attn, y_rec, w_out_bf16, mod, norm2_w, mod, mod)


UP_ROWS = 256


def _conv_rows(stage, half, cols, cw_ref, cb_ref):
    w0, w1, w2 = cw_ref[0:1, cols], cw_ref[1:2, cols], cw_ref[2:3, cols]
    b = cb_ref[:, cols]
    r = [stage[pl.ds(SUBLANES - 2 + k, half, stride=2), :] for k in range(4)]
    even = r[2] * w2 + r[1] * w1 + r[0] * w0 + b
    odd = r[3] * w2 + r[2] * w1 + r[1] * w0 + b
    return even, odd


def _up_kernel(h_ref, wg_ref, wu_ref, cwg_ref, cwu_ref, cbg_ref, cbu_ref, o_ref,
               tailg_ref, tailu_ref, stage_ref, act_ref):
    @pl.when(pl.program_id(1) == 0)
    def _():
        tailg_ref[...] = jnp.zeros_like(tailg_ref)
        tailu_ref[...] = jnp.zeros_like(tailu_ref)

    tm, n = o_ref.shape
    half = tm // 2
    for c in range(n // SUB_N):
        cols = slice(c * SUB_N, (c + 1) * SUB_N)
        pieces = range(0, tm, UP_ROWS)
        g = jnp.concatenate([jnp.dot(h_ref[r0:r0 + UP_ROWS, :], wg_ref[:, cols],
                                     preferred_element_type=F32) for r0 in pieces], axis=0)
        u = jnp.concatenate([jnp.dot(h_ref[r0:r0 + UP_ROWS, :], wu_ref[:, cols],
                                     preferred_element_type=F32) for r0 in pieces], axis=0)
        for lh in range(SUB_N // LANES):
            lanes = slice(c * SUB_N + lh * LANES, c * SUB_N + (lh + 1) * LANES)
            sg = stage_ref.at[c % 2, 0, lh]
            su = stage_ref.at[c % 2, 1, lh]
            sg[0:SUBLANES, :] = tailg_ref[:, lanes]
            su[0:SUBLANES, :] = tailu_ref[:, lanes]
            sg[SUBLANES:, :] = g[:, lh * LANES:(lh + 1) * LANES]
            su[SUBLANES:, :] = u[:, lh * LANES:(lh + 1) * LANES]
            tailg_ref[:, lanes] = sg[tm:, :]
            tailu_ref[:, lanes] = su[tm:, :]
            plane = act_ref.at[lh]
            gates = _conv_rows(sg, half, lanes, cwg_ref, cbg_ref)
            ups = _conv_rows(su, half, lanes, cwu_ref, cbu_ref)
            for parity in range(2):
                plane[pl.ds(parity, half, stride=2), :] = _silu(gates[parity]) * ups[parity]
            o_ref[:, lanes] = plane[...].astype(o_ref.dtype)


def _ffn_up(h2, w_up_bf16, conv_w, conv_b, tm=1024, tn=D_FF // 2):
    s, d = h2.shape
    nj = D_FF // tn
    once = pl.Buffered(1)
    return pl.pallas_call(
        _up_kernel,
        out_shape=jax.ShapeDtypeStruct((s, D_FF), BF16),
        grid=(nj, s // tm),
        in_specs=[pl.BlockSpec((tm, d), lambda j, i: (i, 0)),
                  pl.BlockSpec((d, tn), lambda j, i: (0, j), pipeline_mode=once),
                  pl.BlockSpec((d, tn), lambda j, i: (0, nj + j), pipeline_mode=once),
                  pl.BlockSpec((CONV_WIDTH, tn), lambda j, i: (0, j)),
                  pl.BlockSpec((CONV_WIDTH, tn), lambda j, i: (0, nj + j)),
                  pl.BlockSpec((1, tn), lambda j, i: (0, j)),
                  pl.BlockSpec((1, tn), lambda j, i: (0, nj + j))],
        out_specs=pl.BlockSpec((tm, tn), lambda j, i: (i, j)),
        scratch_shapes=[pltpu.VMEM((SUBLANES, tn), F32),
                        pltpu.VMEM((SUBLANES, tn), F32),
                        pltpu.VMEM((2, 2, SUB_N // LANES, SUBLANES + tm, LANES), F32),
                        pltpu.VMEM((SUB_N // LANES, tm, LANES), F32)],
        compiler_params=_params("parallel", "arbitrary"),
        name="ffn_up",
    )(h2, w_up_bf16, w_up_bf16, conv_w, conv_w, conv_b, conv_b)


def _down_kernel(a_ref, w_ref, x1_ref, g_ref, fw_ref, o_ref):
    y = jnp.dot(a_ref[...], w_ref[...], preferred_element_type=F32)
    x2 = x1_ref[...] + g_ref[...] * y
    ms = jnp.mean(x2 * x2, axis=-1, keepdims=True)
    o_ref[...] = (x2 * lax.rsqrt(ms + EPS)) * fw_ref[...]


def _ffn_down(act, w_down_bf16, x1, mod, final_w, tm=512):
    s, d = x1.shape
    return pl.pallas_call(
        _down_kernel,
        out_shape=jax.ShapeDtypeStruct((s, d), F32),
        grid=(s // tm,),
        in_specs=[pl.BlockSpec((tm, D_FF), lambda i: (i, 0)),
                  pl.BlockSpec((D_FF, d), lambda i: (0, 0), pipeline_mode=pl.Buffered(1)),
                  pl.BlockSpec((tm, d), lambda i: (i, 0)),
                  pl.BlockSpec((1, d), lambda i: (0, 5)),
                  pl.BlockSpec((1, d), lambda i: (0, 0))],
        out_specs=pl.BlockSpec((tm, d), lambda i: (i, 0)),
        compiler_params=_params("parallel"),
        name="ffn_down",
    )(act, w_down_bf16, x1, mod, final_w)


def _rotary_inv_freq():
    inv = 1.0 / (ROPE_THETA ** (jnp.arange(0, ROT_DIM, 2, dtype=F32) / ROT_DIM))
    return jnp.tile(jnp.concatenate([inv, inv]), LANES // ROT_DIM).reshape(1, LANES)


def kernel(x, c, positions, w_ada, b_ada, norm1_w, w_in, lb_logits, gnorm_w, w_out,
           norm2_w, w_up, conv_w, conv_b, w_down, final_norm_w):
    b, s, d = x.shape
    assert b == 1 and d == D_MODEL and w_ada.shape[0] == 1
    x2d = x.reshape(s, d)
    pos_col = positions.reshape(s, 1)

    mod = _modulation(c.reshape(d, 1), w_ada[0], b_ada)
    proj = _in_proj(x2d, pos_col, _rotary_inv_freq(), norm1_w, mod, mod, lb_logits,
                    _arrange_w_in(w_in[0]))
    y_attn = _attention(proj)
    y_rec = _recurrence(proj, gnorm_w)
    x1, h2 = _out_proj(x2d, y_attn, y_rec, w_out[0].astype(BF16), mod, norm2_w)
    act = _ffn_up(h2, w_up[0].astype(BF16), conv_w[0], conv_b)
    out = _ffn_down(act, w_down[0].astype(BF16), x1, mod, final_norm_w.reshape(1, d))
    return out.reshape(b, s, d)
```

```python
import jax
import jax.numpy as jnp
from jax import lax
from jax.experimental import pallas as pl
from jax.experimental.pallas import tpu as pltpu

F32 = jnp.float32
BF16 = jnp.bfloat16

D_MODEL = 2048
N_HEADS = 8
HEAD_DIM = 128
WIDTH = N_HEADS * HEAD_DIM
ROT_DIM = HEAD_DIM // 4
ROPE_THETA = 500000.0
DILATIONS = (1, 4, 16)
WINDOW_STEPS = 128
D_FF = 5632
CONV_WIDTH = 3
EPS = 1e-6
MASKED = -1e30
LOG2_E = 1.4426950408889634

VMEM_LIMIT = 56 * 1024 * 1024
SUB_N = 256
LANES = 128
SUBLANES = 8


def _sigmoid(x):
    return 1.0 / (1.0 + jnp.exp(-x))


def _silu(x):
    return x * _sigmoid(x)


def _params(*semantics):
    return pltpu.CompilerParams(dimension_semantics=semantics,
                                vmem_limit_bytes=VMEM_LIMIT)


def _mod_kernel(c_ref, w_ref, b_ref, o_ref):
    ca = _silu(c_ref[...])
    o_ref[...] = jnp.sum(w_ref[...] * ca, axis=0, keepdims=True) + b_ref[...]


def _modulation(c_col, w_ada, b_ada):
    d, n = w_ada.shape
    tn = 1024
    return pl.pallas_call(
        _mod_kernel,
        out_shape=jax.ShapeDtypeStruct((1, n), F32),
        grid=(n // tn,),
        in_specs=[pl.BlockSpec((d, 1), lambda j: (0, 0)),
                  pl.BlockSpec((d, tn), lambda j: (0, j)),
                  pl.BlockSpec((1, tn), lambda j: (0, j))],
        out_specs=pl.BlockSpec((1, tn), lambda j: (0, j)),
        compiler_params=_params("parallel"),
        name="mod",
    )(c_col, w_ada, b_ada)


def _norm_modulate(x, norm_w, scale, shift):
    ms = jnp.mean(x * x, axis=-1, keepdims=True)
    gain = norm_w * (1.0 + scale)
    return (x * lax.rsqrt(ms + EPS)) * gain + shift


SEG_AQ, SEG_AV, SEG_AK, SEG_HI, SEG_HF, SEG_HQ, SEG_HG = range(7)
SEG_SOURCE = (0, 2, 1, 5, 4, 3, 6)


def _in_kernel(x_ref, pos_ref, invf_ref, nw_ref, sh_ref, sc_ref, lbl_ref, w_ref, o_ref,
               h_ref, cos_ref, sina_ref, sinb_ref):
    h = _norm_modulate(x_ref[...], nw_ref[...], sc_ref[...], sh_ref[...])
    h_ref[...] = h.astype(BF16)

    tm = pos_ref.shape[0]
    groups = LANES // ROT_DIM
    rows = tm // groups
    lane = lax.broadcasted_iota(jnp.int32, (rows, LANES), 1)
    posf = pos_ref[...].astype(F32)
    packed = jnp.broadcast_to(posf[(groups - 1) * rows:], (rows, LANES))
    for g in range(groups - 2, -1, -1):
        packed = jnp.where(lane < (g + 1) * ROT_DIM,
                           jnp.broadcast_to(posf[g * rows:(g + 1) * rows], (rows, LANES)),
                           packed)
    ang = packed * invf_ref[...]
    c = jnp.cos(ang)
    s = jnp.sin(ang)
    half = ROT_DIM // 2
    for g in range(groups):
        cg = pltpu.roll(c, (LANES - g * ROT_DIM) % LANES, 1) if g else c
        sg = pltpu.roll(s, (LANES - g * ROT_DIM) % LANES, 1) if g else s
        dst = slice(g * rows, (g + 1) * rows)
        cos_ref[dst, :] = jnp.where(lane < ROT_DIM, cg, 1.0)
        sina_ref[dst, :] = jnp.where(lane < half, -sg, 0.0)
        sinb_ref[dst, :] = jnp.where((lane >= half) & (lane < ROT_DIM), sg, 0.0)

    def segment(seg, epilogue):
        for c0 in range(0, WIDTH, SUB_N):
            cols = slice(c0, c0 + SUB_N)
            out_cols = slice(seg * WIDTH + c0, seg * WIDTH + c0 + SUB_N)
            r = jnp.dot(h_ref[...], w_ref[:, out_cols], preferred_element_type=F32)
            o_ref[:, out_cols] = epilogue(r, cols)

    def rotary(scale):
        def epilogue(r, cols):
            outs = []
            for hh in range(SUB_N // HEAD_DIM):
                rh = r[:, hh * HEAD_DIM:(hh + 1) * HEAD_DIM]
                out = (rh * cos_ref[...]
                       + pltpu.roll(rh, HEAD_DIM - half, 1) * sina_ref[...]
                       + pltpu.roll(rh, half, 1) * sinb_ref[...])
                outs.append(out * scale if scale != 1.0 else out)
            return jnp.concatenate(outs, axis=1)
        return epilogue

    def plain(r, cols):
        return r

    lg = lbl_ref[...]
    e = jnp.exp(lg - jnp.max(lg, axis=0, keepdims=True))
    lb = e[0:1, :] / jnp.sum(e, axis=0, keepdims=True)

    def log2_forget(r, cols):
        f = lb[:, cols] + (1.0 - lb[:, cols]) * _sigmoid(r)
        return jnp.log2(f)

    segment(SEG_AQ, rotary(HEAD_DIM ** -0.5 * LOG2_E))
    segment(SEG_AV, plain)
    segment(SEG_AK, rotary(1.0))
    segment(SEG_HI, plain)
    segment(SEG_HF, log2_forget)
    segment(SEG_HQ, lambda r, cols: _silu(r) * (HEAD_DIM ** -0.5))
    segment(SEG_HG, lambda r, cols: _silu(r))


def _arrange_w_in(w_in):
    segs = [w_in[:, k * WIDTH:(k + 1) * WIDTH] for k in SEG_SOURCE]
    return jnp.concatenate(segs, axis=1).astype(BF16)


def _in_proj(x2d, pos_col, invf, norm_w, shift, scale, lb_logits, w_in_bf16, tm=256):
    s, d = x2d.shape
    n = w_in_bf16.shape[1]
    return pl.pallas_call(
        _in_kernel,
        out_shape=jax.ShapeDtypeStruct((s, n), F32),
        grid=(s // tm,),
        in_specs=[pl.BlockSpec((tm, d), lambda i: (i, 0)),
                  pl.BlockSpec((tm, 1), lambda i: (i, 0)),
                  pl.BlockSpec((1, HEAD_DIM), lambda i: (0, 0)),
                  pl.BlockSpec((1, d), lambda i: (0, 0)),
                  pl.BlockSpec((1, d), lambda i: (0, 0)),
                  pl.BlockSpec((1, d), lambda i: (0, 1)),
                  pl.BlockSpec(lb_logits.shape, lambda i: (0, 0)),
                  pl.BlockSpec((d, n), lambda i: (0, 0), pipeline_mode=pl.Buffered(1))],
        out_specs=pl.BlockSpec((tm, n), lambda i: (i, 0)),
        scratch_shapes=[pltpu.VMEM((tm, d), BF16),
                        pltpu.VMEM((tm, HEAD_DIM), F32),
                        pltpu.VMEM((tm, HEAD_DIM), F32),
                        pltpu.VMEM((tm, HEAD_DIM), F32)],
        compiler_params=_params("parallel"),
        name="in_proj",
    )(x2d, pos_col, invf, norm_w, shift, scale, lb_logits, w_in_bf16)


ATT_TILE = 2048
ATT_BLK = 128
ATT_P4 = 4


def _attn_kernel(q_ref, k_ref, v_ref, o_ref,
                 q4_ref, k4_ref, v4_ref, k16_ref, v16_ref, ktail_ref, vtail_ref,
                 acc_ref, m_ref, l_ref, y_ref):
    t = pl.program_id(1)
    cur = t % 2
    prev = 1 - cur
    span4 = ATT_TILE // ATT_P4

    @pl.when(t == 0)
    def _():
        k4_ref[1] = jnp.zeros_like(k4_ref[1])
        v4_ref[1] = jnp.zeros_like(v4_ref[1])
        k16_ref[1] = jnp.zeros_like(k16_ref[1])
        v16_ref[1] = jnp.zeros_like(v16_ref[1])
        ktail_ref[...] = jnp.zeros_like(ktail_ref)
        vtail_ref[...] = jnp.zeros_like(vtail_ref)

    qi = lax.broadcasted_iota(jnp.int32, (ATT_BLK, 2 * ATT_BLK), 0)
    ki = lax.broadcasted_iota(jnp.int32, (ATT_BLK, 2 * ATT_BLK), 1)
    band = (ki >= qi) & (ki <= qi + WINDOW_STEPS)
    band_first = band & ((ki >= ATT_BLK) | (t > 0))
    nt_dims = (((1,), (1,)), ((), ()))
    ones = jnp.ones((2 * ATT_BLK, HEAD_DIM), BF16)

    def block(qb, kk, vv, valid):
        s = lax.dot_general(qb, kk, nt_dims, preferred_element_type=F32)
        s = jnp.where(valid, s, MASKED)
        m = jnp.max(s, axis=1, keepdims=True)
        pe = jnp.exp2(s - m).astype(BF16)
        ext = jnp.dot(pe, jnp.concatenate([vv, ones], axis=1), preferred_element_type=F32)
        return (ext[:, :HEAD_DIM], jnp.broadcast_to(m, (ATT_BLK, HEAD_DIM)),
                ext[:, HEAD_DIM:])

    def put(p, rows, res):
        acc_ref[p, rows, :], m_ref[p, rows, :], l_ref[p, rows, :] = res

    for r in range(ATT_P4):
        src = pl.ds(r, span4, stride=ATT_P4)
        dst = pl.ds(r * span4, span4)
        q4_ref[dst, :] = q_ref[src, :]
        k4_ref[cur, dst, :] = k_ref[src, :]
        v4_ref[cur, dst, :] = v_ref[src, :]

    for b in range(ATT_TILE // ATT_BLK):
        rows = pl.ds(b * ATT_BLK, ATT_BLK)
        qb = q_ref[rows, :].astype(BF16)
        if b > 0:
            both = pl.ds((b - 1) * ATT_BLK, 2 * ATT_BLK)
            kk = k_ref[both, :].astype(BF16)
            vv = v_ref[both, :].astype(BF16)
            valid = band
        else:
            kk = jnp.concatenate([ktail_ref[...], k_ref[rows, :].astype(BF16)], axis=0)
            vv = jnp.concatenate([vtail_ref[...], v_ref[rows, :].astype(BF16)], axis=0)
            valid = band_first
        put(0, rows, block(qb, kk, vv, valid))
    last = pl.ds(ATT_TILE - ATT_BLK, ATT_BLK)
    ktail_ref[...] = k_ref[last, :].astype(BF16)
    vtail_ref[...] = v_ref[last, :].astype(BF16)

    for r in range(ATT_P4):
        for b in range(span4 // ATT_BLK):
            base = r * span4 + b * ATT_BLK
            rows = pl.ds(base, ATT_BLK)
            qb = q4_ref[rows, :].astype(BF16)
            if b > 0:
                both = pl.ds(base - ATT_BLK, 2 * ATT_BLK)
                kk = k4_ref[cur, both, :].astype(BF16)
                vv = v4_ref[cur, both, :].astype(BF16)
                valid = band
            else:
                back = pl.ds((r + 1) * span4 - ATT_BLK, ATT_BLK)
                kk = jnp.concatenate([k4_ref[prev, back, :].astype(BF16),
                                      k4_ref[cur, rows, :].astype(BF16)], axis=0)
                vv = jnp.concatenate([v4_ref[prev, back, :].astype(BF16),
                                      v4_ref[cur, rows, :].astype(BF16)], axis=0)
                valid = band_first
            put(1, rows, block(qb, kk, vv, valid))

    for r in range(ATT_P4):
        for r2 in range(ATT_P4):
            src = pl.ds(r * span4 + r2, ATT_BLK, stride=ATT_P4)
            dst = pl.ds((ATT_P4 * r2 + r) * ATT_BLK, ATT_BLK)
            qb = q4_ref[src, :].astype(BF16)
            kc = k4_ref[cur, src, :].astype(BF16)
            vc = v4_ref[cur, src, :].astype(BF16)
            k16_ref[cur, dst, :] = kc
            v16_ref[cur, dst, :] = vc
            kk = jnp.concatenate([k16_ref[prev, dst, :], kc], axis=0)
            vv = jnp.concatenate([v16_ref[prev, dst, :], vc], axis=0)
            put(2, src, block(qb, kk, vv, band_first))

    for r in range(ATT_P4):
        for b in range(span4 // ATT_BLK):
            rows4 = pl.ds(r * span4 + b * ATT_BLK, ATT_BLK)
            nat = pl.ds(r + ATT_P4 * b * ATT_BLK, ATT_BLK, stride=ATT_P4)
            ms = [m_ref[0, nat, :], m_ref[1, rows4, :], m_ref[2, rows4, :]]
            ls = [l_ref[0, nat, :], l_ref[1, rows4, :], l_ref[2, rows4, :]]
            accs = [acc_ref[0, nat, :], acc_ref[1, rows4, :], acc_ref[2, rows4, :]]
            m_all = jnp.maximum(jnp.maximum(ms[0], ms[1]), ms[2])
            num = jnp.zeros((ATT_BLK, HEAD_DIM), F32)
            den = jnp.zeros((ATT_BLK, HEAD_DIM), F32)
            for p in range(len(DILATIONS)):
                w = jnp.exp2(ms[p] - m_all)
                num = num + w * accs[p]
                den = den + w * ls[p]
            y_ref[nat, :] = num / den
    o_ref[...] = y_ref[...].astype(o_ref.dtype)


def _attention(proj):
    s = proj.shape[0]
    nt = s // ATT_TILE
    blk = (ATT_TILE, HEAD_DIM)
    n_pat = len(DILATIONS)
    return pl.pallas_call(
        _attn_kernel,
        out_shape=jax.ShapeDtypeStruct((s, WIDTH), BF16),
        grid=(N_HEADS, nt),
        in_specs=[pl.BlockSpec(blk, lambda h, t: (t, SEG_AQ * N_HEADS + h)),
                  pl.BlockSpec(blk, lambda h, t: (t, SEG_AK * N_HEADS + h)),
                  pl.BlockSpec(blk, lambda h, t: (t, SEG_AV * N_HEADS + h))],
        out_specs=pl.BlockSpec(blk, lambda h, t: (t, h)),
        scratch_shapes=[pltpu.VMEM(blk, F32),
                        pltpu.VMEM((2,) + blk, F32),
                        pltpu.VMEM((2,) + blk, F32),
                        pltpu.VMEM((2,) + blk, BF16),
                        pltpu.VMEM((2,) + blk, BF16),
                        pltpu.VMEM((ATT_BLK, HEAD_DIM), BF16),
                        pltpu.VMEM((ATT_BLK, HEAD_DIM), BF16),
                        pltpu.VMEM((n_pat,) + blk, F32),
                        pltpu.VMEM((n_pat,) + blk, F32),
                        pltpu.VMEM((n_pat,) + blk, F32),
                        pltpu.VMEM(blk, F32)],
        compiler_params=_params("parallel", "arbitrary"),
        name="attn",
    )(proj, proj, proj)


REC_CHUNK = 128
REC_LEVELS = 7
REC_ROWS = 2048
REC_HEADS = 4


def _block_scans(lf):
    c = lf.shape[0]
    fwd, rev = [lf], [lf]
    x, y = lf, lf
    for j in range(REC_LEVELS):
        m = 1 << j
        if m < SUBLANES:
            x3 = x.reshape(c // SUBLANES, SUBLANES, HEAD_DIM)
            y3 = y.reshape(c // SUBLANES, SUBLANES, HEAD_DIM)
            sub = lax.broadcasted_iota(jnp.int32, x3.shape, 1)
            block0 = sub & ~(2 * m - 1)
            upper = (sub & m) != 0
            addx = jnp.where(upper, jnp.take_along_axis(x3, block0 + (m - 1), axis=1), 0.0)
            addy = jnp.where(upper, 0.0, jnp.take_along_axis(y3, block0 + m, axis=1))
            x = x + addx.reshape(c, HEAD_DIM)
            y = y + addy.reshape(c, HEAD_DIM)
        else:
            xs, ys = [], []
            for b in range(c // (2 * m)):
                lo = x[b * 2 * m: b * 2 * m + m]
                hi = x[b * 2 * m + m: (b + 1) * 2 * m]
                xs += [lo, hi + lo[m - 1:m, :]]
                lo = y[b * 2 * m: b * 2 * m + m]
                hi = y[b * 2 * m + m: (b + 1) * 2 * m]
                ys += [lo + hi[0:1, :], hi]
            x = jnp.concatenate(xs, axis=0)
            y = jnp.concatenate(ys, axis=0)
        fwd.append(x)
        rev.append(y)
    return fwd, rev


def _rec_kernel(q_ref, lf_ref, v_ref, g_ref, gw_ref, o_ref, state_ref):
    @pl.when(pl.program_id(1) == 0)
    def _():
        state_ref[...] = jnp.zeros_like(state_ref)

    c = REC_CHUNK
    ti = lax.broadcasted_iota(jnp.int32, (c, c), 0)
    si = lax.broadcasted_iota(jnp.int32, (c, c), 1)
    diff = ti ^ si
    level = jnp.full((c, c), -1, jnp.int32)
    for j in range(REC_LEVELS):
        level = jnp.where((diff >> j) == 1, j, level)
    level = jnp.where(si > ti, -1, jnp.where(si == ti, REC_LEVELS, level))
    nt_dims = (((1,), (1,)), ((), ()))

    def one_head(hh, rows):
        lanes = slice(hh * HEAD_DIM, (hh + 1) * HEAD_DIM)
        qf = q_ref[rows, lanes]
        lf = lf_ref[rows, lanes]
        v = v_ref[rows, lanes].astype(BF16)
        kk = 1.0 - jnp.exp2(lf)
        fwd, rev = _block_scans(lf)
        b = fwd[REC_LEVELS]

        a = jnp.where(level == REC_LEVELS,
                      lax.dot_general(qf.astype(BF16), kk.astype(BF16), nt_dims,
                                      preferred_element_type=F32), 0.0)
        for j in range(REC_LEVELS):
            m = 1 << j
            if m < SUBLANES:
                ql = (qf * jnp.exp2(fwd[j])).astype(BF16)
                kl = (kk * jnp.exp2(rev[j] - lf)).astype(BF16)
                al = lax.dot_general(ql, kl, nt_dims, preferred_element_type=F32)
                a = jnp.where(level == j, al, a)
                continue
            lows = [slice(b, b + m) for b in range(0, c, 2 * m)]
            ups = [slice(b + m, b + 2 * m) for b in range(0, c, 2 * m)]
            pick = lambda x, parts: jnp.concatenate([x[p] for p in parts], axis=0)
            ql = (pick(qf, ups) * jnp.exp2(pick(fwd[j], ups))).astype(BF16)
            k_low = pick(kk, lows) * jnp.exp2(pick(rev[j], lows) - pick(lf, lows))
            zeros = jnp.zeros((m, HEAD_DIM), F32)
            kl = jnp.concatenate(
                [piece for b in range(len(lows))
                 for piece in (k_low[b * m:(b + 1) * m], zeros)], axis=0).astype(BF16)
            al = lax.dot_general(ql, kl, nt_dims, preferred_element_type=F32)
            a_up = jnp.where(pick(level, ups) == j, al, pick(a, ups))
            a = jnp.concatenate(
                [piece for b in range(len(lows))
                 for piece in (a[lows[b]], a_up[b * m:(b + 1) * m])], axis=0)

        state_t = state_ref[hh]
        o = (lax.dot_general((qf * jnp.exp2(b)).astype(BF16), state_t.astype(BF16),
                             nt_dims, preferred_element_type=F32)
             + jnp.dot(a.astype(BF16), v, preferred_element_type=F32))
        b_last = b[c - 1:c, :]
        kd = (kk * jnp.exp2(b_last - b)).astype(BF16)
        upd_t = lax.dot_general(v, kd, (((0,), (0,)), ((), ())),
                                preferred_element_type=F32)
        state_ref[hh] = jnp.exp2(b_last) * state_t + upd_t

        o = o * lax.rsqrt(jnp.mean(o * o, axis=-1, keepdims=True) + EPS) * gw_ref[...]
        o_ref[rows, lanes] = (o * g_ref[rows, lanes]).astype(o_ref.dtype)

    def chunk(ci, carry):
        rows = pl.ds(pl.multiple_of(ci * c, c), c)
        for hh in range(REC_HEADS):
            one_head(hh, rows)
        return carry

    lax.fori_loop(0, REC_ROWS // c, chunk, 0, unroll=2)


def _recurrence(proj, gnorm_w):
    s = proj.shape[0]
    blk = (REC_ROWS, REC_HEADS * HEAD_DIM)
    groups = N_HEADS // REC_HEADS
    seg = lambda k: (lambda h, i: (i, k * groups + h))
    return pl.pallas_call(
        _rec_kernel,
        out_shape=jax.ShapeDtypeStruct((s, WIDTH), BF16),
        grid=(groups, s // REC_ROWS),
        in_specs=[pl.BlockSpec(blk, seg(SEG_HQ)),
                  pl.BlockSpec(blk, seg(SEG_HF)),
                  pl.BlockSpec(blk, seg(SEG_HI)),
                  pl.BlockSpec(blk, seg(SEG_HG)),
                  pl.BlockSpec((1, HEAD_DIM), lambda h, i: (0, 0))],
        out_specs=pl.BlockSpec(blk, lambda h, i: (i, h)),
        scratch_shapes=[pltpu.VMEM((REC_HEADS, HEAD_DIM, HEAD_DIM), F32)],
        compiler_params=_params("parallel", "arbitrary"),
        name="hgrn",
    )(proj, proj, proj, proj, gnorm_w)


OUT_ROWS = 256


def _out_kernel(x_ref, ya_ref, yr_ref, w_ref, g_ref, nw_ref, sh_ref, sc_ref,
                x1_ref, h2_ref):
    for r0 in range(0, x_ref.shape[0], OUT_ROWS):
        rows = slice(r0, r0 + OUT_ROWS)
        y = (jnp.dot(ya_ref[rows, :], w_ref[0:WIDTH, :], preferred_element_type=F32)
             + jnp.dot(yr_ref[rows, :], w_ref[WIDTH:, :], preferred_element_type=F32))
        x1 = x_ref[rows, :] + g_ref[...] * y
        x1_ref[rows, :] = x1
        h2_ref[rows, :] = _norm_modulate(x1, nw_ref[...], sc_ref[...], sh_ref[...]).astype(BF16)


def _out_proj(x2d, y_attn, y_rec, w_out_bf16, mod, norm2_w, tm=512):
    s, d = x2d.shape
    vec = lambda k: pl.BlockSpec((1, d), lambda i: (0, k))
    return pl.pallas_call(
        _out_kernel,
        out_shape=(jax.ShapeDtypeStruct((s, d), F32),
                   jax.ShapeDtypeStruct((s, d), BF16)),
        grid=(s // tm,),
        in_specs=[pl.BlockSpec((tm, d), lambda i: (i, 0)),
                  pl.BlockSpec((tm, WIDTH), lambda i: (i, 0)),
                  pl.BlockSpec((tm, WIDTH), lambda i: (i, 0)),
                  pl.BlockSpec((2 * WIDTH, d), lambda i: (0, 0)),
                  vec(2),
                  pl.BlockSpec((1, d), lambda i: (0, 0)),
                  vec(3), vec(4)],
        out_specs=(pl.BlockSpec((tm, d), lambda i: (i, 0)),
                   pl.BlockSpec((tm, d), lambda i: (i, 0))),
        compiler_params=_params("parallel"),
        name="out_proj",
    )(x2d, y_attn, y_rec, w_out_bf16, mod, norm2_w, mod, mod)


UP_ROWS = 256


def _conv_rows(stage, half, cols, cw_ref, cb_ref):
    w0, w1, w2 = cw_ref[0:1, cols], cw_ref[1:2, cols], cw_ref[2:3, cols]
    b = cb_ref[:, cols]
    r = [stage[pl.ds(SUBLANES - 2 + k, half, stride=2), :] for k in range(4)]
    even = r[2] * w2 + r[1] * w1 + r[0] * w0 + b
    odd = r[3] * w2 + r[2] * w1 + r[1] * w0 + b
    return even, odd


def _up_kernel(h_ref, wg_ref, wu_ref, cwg_ref, cwu_ref, cbg_ref, cbu_ref, o_ref,
               tailg_ref, tailu_ref, stage_ref, act_ref):
    @pl.when(pl.program_id(1) == 0)
    def _():
        tailg_ref[...] = jnp.zeros_like(tailg_ref)
        tailu_ref[...] = jnp.zeros_like(tailu_ref)

    tm, n = o_ref.shape
    half = tm // 2
    for c in range(n // SUB_N):
        cols = slice(c * SUB_N, (c + 1) * SUB_N)
        pieces = range(0, tm, UP_ROWS)
        g = jnp.concatenate([jnp.dot(h_ref[r0:r0 + UP_ROWS, :], wg_ref[:, cols],
                                     preferred_element_type=F32) for r0 in pieces], axis=0)
        u = jnp.concatenate([jnp.dot(h_ref[r0:r0 + UP_ROWS, :], wu_ref[:, cols],
                                     preferred_element_type=F32) for r0 in pieces], axis=0)
        for lh in range(SUB_N // LANES):
            lanes = slice(c * SUB_N + lh * LANES, c * SUB_N + (lh + 1) * LANES)
            sg = stage_ref.at[c % 2, 0, lh]
            su = stage_ref.at[c % 2, 1, lh]
            sg[0:SUBLANES, :] = tailg_ref[:, lanes]
            su[0:SUBLANES, :] = tailu_ref[:, lanes]
            sg[SUBLANES:, :] = g[:, lh * LANES:(lh + 1) * LANES]
            su[SUBLANES:, :] = u[:, lh * LANES:(lh + 1) * LANES]
            tailg_ref[:, lanes] = sg[tm:, :]
            tailu_ref[:, lanes] = su[tm:, :]
            plane = act_ref.at[lh]
            gates = _conv_rows(sg, half, lanes, cwg_ref, cbg_ref)
            ups = _conv_rows(su, half, lanes, cwu_ref, cbu_ref)
            for parity in range(2):
                plane[pl.ds(parity, half, stride=2), :] = _silu(gates[parity]) * ups[parity]
            o_ref[:, lanes] = plane[...].astype(o_ref.dtype)


def _ffn_up(h2, w_up_bf16, conv_w, conv_b, tm=1024, tn=D_FF // 2):
    s, d = h2.shape
    nj = D_FF // tn
    once = pl.Buffered(1)
    return pl.pallas_call(
        _up_kernel,
        out_shape=jax.ShapeDtypeStruct((s, D_FF), BF16),
        grid=(nj, s // tm),
        in_specs=[pl.BlockSpec((tm, d), lambda j, i: (i, 0)),
                  pl.BlockSpec((d, tn), lambda j, i: (0, j), pipeline_mode=once),
                  pl.BlockSpec((d, tn), lambda j, i: (0, nj + j), pipeline_mode=once),
                  pl.BlockSpec((CONV_WIDTH, tn), lambda j, i: (0, j)),
                  pl.BlockSpec((CONV_WIDTH, tn), lambda j, i: (0, nj + j)),
                  pl.BlockSpec((1, tn), lambda j, i: (0, j)),
                  pl.BlockSpec((1, tn), lambda j, i: (0, nj + j))],
        out_specs=pl.BlockSpec((tm, tn), lambda j, i: (i, j)),
        scratch_shapes=[pltpu.VMEM((SUBLANES, tn), F32),
                        pltpu.VMEM((SUBLANES, tn), F32),
                        pltpu.VMEM((2, 2, SUB_N // LANES, SUBLANES + tm, LANES), F32),
                        pltpu.VMEM((SUB_N // LANES, tm, LANES), F32)],
        compiler_params=_params("parallel", "arbitrary"),
        name="ffn_up",
    )(h2, w_up_bf16, w_up_bf16, conv_w, conv_w, conv_b, conv_b)


def _down_kernel(a_ref, w_ref, x1_ref, g_ref, fw_ref, o_ref):
    y = jnp.dot(a_ref[...], w_ref[...], preferred_element_type=F32)
    x2 = x1_ref[...] + g_ref[...] * y
    ms = jnp.mean(x2 * x2, axis=-1, keepdims=True)
    o_ref[...] = (x2 * lax.rsqrt(ms + EPS)) * fw_ref[...]


def _ffn_down(act, w_down_bf16, x1, mod, final_w, tm=512):
    s, d = x1.shape
    return pl.pallas_call(
        _down_kernel,
        out_shape=jax.ShapeDtypeStruct((s, d), F32),
        grid=(s // tm,),
        in_specs=[pl.BlockSpec((tm, D_FF), lambda i: (i, 0)),
                  pl.BlockSpec((D_FF, d), lambda i: (0, 0), pipeline_mode=pl.Buffered(1)),
                  pl.BlockSpec((tm, d), lambda i: (i, 0)),
                  pl.BlockSpec((1, d), lambda i: (0, 5)),
                  pl.BlockSpec((1, d), lambda i: (0, 0))],
        out_specs=pl.BlockSpec((tm, d), lambda i: (i, 0)),
        compiler_params=_params("parallel"),
        name="ffn_down",
    )(act, w_down_bf16, x1, mod, final_w)


def _rotary_inv_freq():
    inv = 1.0 / (ROPE_THETA ** (jnp.arange(0, ROT_DIM, 2, dtype=F32) / ROT_DIM))
    return jnp.tile(jnp.concatenate([inv, inv]), LANES // ROT_DIM).reshape(1, LANES)


def kernel(x, c, positions, w_ada, b_ada, norm1_w, w_in, lb_logits, gnorm_w, w_out,
           norm2_w, w_up, conv_w, conv_b, w_down, final_norm_w):
    b, s, d = x.shape
    assert b == 1 and d == D_MODEL and w_ada.shape[0] == 1
    x2d = x.reshape(s, d)
    pos_col = positions.reshape(s, 1)

    mod = _modulation(c.reshape(d, 1), w_ada[0], b_ada)
    proj = _in_proj(x2d, pos_col, _rotary_inv_freq(), norm1_w, mod, mod, lb_logits,
                    _arrange_w_in(w_in[0]))
    y_attn = _attention(proj)
    y_rec = _recurrence(proj, gnorm_w)
    x1, h2 = _out_proj(x2d, y_attn, y_rec, w_out[0].astype(BF16), mod, norm2_w)
    act = _ffn_up(h2, w_up[0].astype(BF16), conv_w[0], conv_b)
    out = _ffn_down(act, w_down[0].astype(BF16), x1, mod, final_norm_w.reshape(1, d))
    return out.reshape(b, s, d)
```

```python
import jax
import jax.numpy as jnp
from jax import lax
from jax.experimental import pallas as pl
from jax.experimental.pallas import tpu as pltpu

F32 = jnp.float32
BF16 = jnp.bfloat16

D_MODEL = 2048
N_HEADS = 8
HEAD_DIM = 128
WIDTH = N_HEADS * HEAD_DIM
ROT_DIM = HEAD_DIM // 4
ROPE_THETA = 500000.0
DILATIONS = (1, 4, 16)
WINDOW_STEPS = 128
D_FF = 5632
CONV_WIDTH = 3
EPS = 1e-6
MASKED = -1e30
LOG2_E = 1.4426950408889634

VMEM_LIMIT = 56 * 1024 * 1024
SUB_N = 256
LANES = 128
SUBLANES = 8


def _sigmoid(x):
    return 1.0 / (1.0 + jnp.exp(-x))


def _silu(x):
    return x * _sigmoid(x)


def _params(*semantics):
    return pltpu.CompilerParams(dimension_semantics=semantics,
                                vmem_limit_bytes=VMEM_LIMIT)


def _mod_kernel(c_ref, w_ref, b_ref, o_ref):
    ca = _silu(c_ref[...])
    o_ref[...] = jnp.sum(w_ref[...] * ca, axis=0, keepdims=True) + b_ref[...]


def _modulation(c_col, w_ada, b_ada):
    d, n = w_ada.shape
    tn = 1024
    return pl.pallas_call(
        _mod_kernel,
        out_shape=jax.ShapeDtypeStruct((1, n), F32),
        grid=(n // tn,),
        in_specs=[pl.BlockSpec((d, 1), lambda j: (0, 0)),
                  pl.BlockSpec((d, tn), lambda j: (0, j)),
                  pl.BlockSpec((1, tn), lambda j: (0, j))],
        out_specs=pl.BlockSpec((1, tn), lambda j: (0, j)),
        compiler_params=_params("parallel"),
        name="mod",
    )(c_col, w_ada, b_ada)


def _norm_modulate(x, norm_w, scale, shift):
    ms = jnp.mean(x * x, axis=-1, keepdims=True)
    gain = norm_w * (1.0 + scale)
    return (x * lax.rsqrt(ms + EPS)) * gain + shift


SEG_AQ, SEG_AV, SEG_AK, SEG_HI, SEG_HF, SEG_HQ, SEG_HG = range(7)
SEG_SOURCE = (0, 2, 1, 5, 4, 3, 6)


def _in_kernel(x_ref, pos_ref, invf_ref, nw_ref, sh_ref, sc_ref, lbl_ref, w_ref, o_ref,
               h_ref, cos_ref, sina_ref, sinb_ref):
    h = _norm_modulate(x_ref[...], nw_ref[...], sc_ref[...], sh_ref[...])
    h_ref[...] = h.astype(BF16)

    tm = pos_ref.shape[0]
    groups = LANES // ROT_DIM
    rows = tm // groups
    lane = lax.broadcasted_iota(jnp.int32, (rows, LANES), 1)
    posf = pos_ref[...].astype(F32)
    packed = jnp.broadcast_to(posf[(groups - 1) * rows:], (rows, LANES))
    for g in range(groups - 2, -1, -1):
        packed = jnp.where(lane < (g + 1) * ROT_DIM,
                           jnp.broadcast_to(posf[g * rows:(g + 1) * rows], (rows, LANES)),
                           packed)
    ang = packed * invf_ref[...]
    c = jnp.cos(ang)
    s = jnp.sin(ang)
    half = ROT_DIM // 2
    for g in range(groups):
        cg = pltpu.roll(c, (LANES - g * ROT_DIM) % LANES, 1) if g else c
        sg = pltpu.roll(s, (LANES - g * ROT_DIM) % LANES, 1) if g else s
        dst = slice(g * rows, (g + 1) * rows)
        cos_ref[dst, :] = jnp.where(lane < ROT_DIM, cg, 1.0)
        sina_ref[dst, :] = jnp.where(lane < half, -sg, 0.0)
        sinb_ref[dst, :] = jnp.where((lane >= half) & (lane < ROT_DIM), sg, 0.0)

    def segment(seg, epilogue):
        for c0 in range(0, WIDTH, SUB_N):
            cols = slice(c0, c0 + SUB_N)
            out_cols = slice(seg * WIDTH + c0, seg * WIDTH + c0 + SUB_N)
            r = jnp.dot(h_ref[...], w_ref[:, out_cols], preferred_element_type=F32)
            o_ref[:, out_cols] = epilogue(r, cols)

    def rotary(scale):
        def epilogue(r, cols):
            outs = []
            for hh in range(SUB_N // HEAD_DIM):
                rh = r[:, hh * HEAD_DIM:(hh + 1) * HEAD_DIM]
                out = (rh * cos_ref[...]
                       + pltpu.roll(rh, HEAD_DIM - half, 1) * sina_ref[...]
                       + pltpu.roll(rh, half, 1) * sinb_ref[...])
                outs.append(out * scale if scale != 1.0 else out)
            return jnp.concatenate(outs, axis=1)
        return epilogue

    def plain(r, cols):
        return r

    lg = lbl_ref[...]
    e = jnp.exp(lg - jnp.max(lg, axis=0, keepdims=True))
    lb = e[0:1, :] / jnp.sum(e, axis=0, keepdims=True)

    def log2_forget(r, cols):
        f = lb[:, cols] + (1.0 - lb[:, cols]) * _sigmoid(r)
        return jnp.log2(f)

    segment(SEG_AQ, rotary(HEAD_DIM ** -0.5 * LOG2_E))
    segment(SEG_AV, plain)
    segment(SEG_AK, rotary(1.0))
    segment(SEG_HI, plain)
    segment(SEG_HF, log2_forget)
    segment(SEG_HQ, lambda r, cols: _silu(r) * (HEAD_DIM ** -0.5))
    segment(SEG_HG, lambda r, cols: _silu(r))


def _arrange_w_in(w_in):
    segs = [w_in[:, k * WIDTH:(k + 1) * WIDTH] for k in SEG_SOURCE]
    return jnp.concatenate(segs, axis=1).astype(BF16)


def _in_proj(x2d, pos_col, invf, norm_w, shift, scale, lb_logits, w_in_bf16, tm=256):
    s, d = x2d.shape
    n = w_in_bf16.shape[1]
    return pl.pallas_call(
        _in_kernel,
        out_shape=jax.ShapeDtypeStruct((s, n), F32),
        grid=(s // tm,),
        in_specs=[pl.BlockSpec((tm, d), lambda i: (i, 0)),
                  pl.BlockSpec((tm, 1), lambda i: (i, 0)),
                  pl.BlockSpec((1, HEAD_DIM), lambda i: (0, 0)),
                  pl.BlockSpec((1, d), lambda i: (0, 0)),
                  pl.BlockSpec((1, d), lambda i: (0, 0)),
                  pl.BlockSpec((1, d), lambda i: (0, 1)),
                  pl.BlockSpec(lb_logits.shape, lambda i: (0, 0)),
                  pl.BlockSpec((d, n), lambda i: (0, 0), pipeline_mode=pl.Buffered(1))],
        out_specs=pl.BlockSpec((tm, n), lambda i: (i, 0)),
        scratch_shapes=[pltpu.VMEM((tm, d), BF16),
                        pltpu.VMEM((tm, HEAD_DIM), F32),
                        pltpu.VMEM((tm, HEAD_DIM), F32),
                        pltpu.VMEM((tm, HEAD_DIM), F32)],
        compiler_params=_params("parallel"),
        name="in_proj",
    )(x2d, pos_col, invf, norm_w, shift, scale, lb_logits, w_in_bf16)


ATT_TILE = 2048
ATT_BLK = 128
ATT_P4 = 4


def _attn_kernel(q_ref, k_ref, v_ref, o_ref,
                 q4_ref, k4_ref, v4_ref, k16_ref, v16_ref, ktail_ref, vtail_ref,
                 acc_ref, m_ref, l_ref, y_ref, bias_ref):
    t = pl.program_id(1)
    cur = t % 2
    prev = 1 - cur
    span4 = ATT_TILE // ATT_P4

    @pl.when(t == 0)
    def _():
        k4_ref[1] = jnp.zeros_like(k4_ref[1])
        v4_ref[1] = jnp.zeros_like(v4_ref[1])
        k16_ref[1] = jnp.zeros_like(k16_ref[1])
        v16_ref[1] = jnp.zeros_like(v16_ref[1])
        ktail_ref[...] = jnp.zeros_like(ktail_ref)
        vtail_ref[...] = jnp.zeros_like(vtail_ref)

    qi = lax.broadcasted_iota(jnp.int32, (ATT_BLK, 2 * ATT_BLK), 0)
    ki = lax.broadcasted_iota(jnp.int32, (ATT_BLK, 2 * ATT_BLK), 1)
    band = (ki >= qi) & (ki <= qi + WINDOW_STEPS)
    band_first = band & ((ki >= ATT_BLK) | (t > 0))
    bias_ref[0] = jnp.where(band, 0.0, MASKED)
    bias_ref[1] = jnp.where(band_first, 0.0, MASKED)
    band, band_first = 0, 1
    nt_dims = (((1,), (1,)), ((), ()))
    ones = jnp.ones((2 * ATT_BLK, HEAD_DIM), BF16)

    def block(qb, kk, vv, valid):
        s = lax.dot_general(qb, kk, nt_dims, preferred_element_type=F32) + bias_ref[valid]
        m = jnp.max(s, axis=1, keepdims=True)
        pe = jnp.exp2(s - m).astype(BF16)
        ext = jnp.dot(pe, jnp.concatenate([vv, ones], axis=1), preferred_element_type=F32)
        return (ext[:, :HEAD_DIM], jnp.broadcast_to(m, (ATT_BLK, HEAD_DIM)),
                ext[:, HEAD_DIM:])

    def put(p, rows, res):
        acc_ref[p, rows, :], m_ref[p, rows, :], l_ref[p, rows, :] = res

    for r in range(ATT_P4):
        src = pl.ds(r, span4, stride=ATT_P4)
        dst = pl.ds(r * span4, span4)
        q4_ref[dst, :] = q_ref[src, :]
        k4_ref[cur, dst, :] = k_ref[src, :]
        v4_ref[cur, dst, :] = v_ref[src, :]

    for b in range(ATT_TILE // ATT_BLK):
        rows = pl.ds(b * ATT_BLK, ATT_BLK)
        qb = q_ref[rows, :].astype(BF16)
        if b > 0:
            both = pl.ds((b - 1) * ATT_BLK, 2 * ATT_BLK)
            kk = k_ref[both, :].astype(BF16)
            vv = v_ref[both, :].astype(BF16)
            valid = band
        else:
            kk = jnp.concatenate([ktail_ref[...], k_ref[rows, :].astype(BF16)], axis=0)
            vv = jnp.concatenate([vtail_ref[...], v_ref[rows, :].astype(BF16)], axis=0)
            valid = band_first
        put(0, rows, block(qb, kk, vv, valid))
    last = pl.ds(ATT_TILE - ATT_BLK, ATT_BLK)
    ktail_ref[...] = k_ref[last, :].astype(BF16)
    vtail_ref[...] = v_ref[last, :].astype(BF16)

    for r in range(ATT_P4):
        for b in range(span4 // ATT_BLK):
            base = r * span4 + b * ATT_BLK
            rows = pl.ds(base, ATT_BLK)
            qb = q4_ref[rows, :].astype(BF16)
            if b > 0:
                both = pl.ds(base - ATT_BLK, 2 * ATT_BLK)
                kk = k4_ref[cur, both, :].astype(BF16)
                vv = v4_ref[cur, both, :].astype(BF16)
                valid = band
            else:
                back = pl.ds((r + 1) * span4 - ATT_BLK, ATT_BLK)
                kk = jnp.concatenate([k4_ref[prev, back, :].astype(BF16),
                                      k4_ref[cur, rows, :].astype(BF16)], axis=0)
                vv = jnp.concatenate([v4_ref[prev, back, :].astype(BF16),
                                      v4_ref[cur, rows, :].astype(BF16)], axis=0)
                valid = band_first
            put(1, rows, block(qb, kk, vv, valid))

    for r in range(ATT_P4):
        for r2 in range(ATT_P4):
            src = pl.ds(r * span4 + r2, ATT_BLK, stride=ATT_P4)
            dst = pl.ds((ATT_P4 * r2 + r) * ATT_BLK, ATT_BLK)
            qb = q4_ref[src, :].astype(BF16)
            kc = k4_ref[cur, src, :].astype(BF16)
            vc = v4_ref[cur, src, :].astype(BF16)
            k16_ref[cur, dst, :] = kc
            v16_ref[cur, dst, :] = vc
            kk = jnp.concatenate([k16_ref[prev, dst, :], kc], axis=0)
            vv = jnp.concatenate([v16_ref[prev, dst, :], vc], axis=0)
            put(2, src, block(qb, kk, vv, band_first))

    for r in range(ATT_P4):
        for b in range(span4 // ATT_BLK):
            rows4 = pl.ds(r * span4 + b * ATT_BLK, ATT_BLK)
            nat = pl.ds(r + ATT_P4 * b * ATT_BLK, ATT_BLK, stride=ATT_P4)
            ms = [m_ref[0, nat, :], m_ref[1, rows4, :], m_ref[2, rows4, :]]
            ls = [l_ref[0, nat, :], l_ref[1, rows4, :], l_ref[2, rows4, :]]
            accs = [acc_ref[0, nat, :], acc_ref[1, rows4, :], acc_ref[2, rows4, :]]
            m_all = jnp.maximum(jnp.maximum(ms[0], ms[1]), ms[2])
            num = jnp.zeros((ATT_BLK, HEAD_DIM), F32)
            den = jnp.zeros((ATT_BLK, HEAD_DIM), F32)
            for p in range(len(DILATIONS)):
                w = jnp.exp2(ms[p] - m_all)
                num = num + w * accs[p]
                den = den + w * ls[p]
            y_ref[nat, :] = num / den
    o_ref[...] = y_ref[...].astype(o_ref.dtype)


def _attention(proj):
    s = proj.shape[0]
    nt = s // ATT_TILE
    blk = (ATT_TILE, HEAD_DIM)
    n_pat = len(DILATIONS)
    return pl.pallas_call(
        _attn_kernel,
        out_shape=jax.ShapeDtypeStruct((s, WIDTH), BF16),
        grid=(N_HEADS, nt),
        in_specs=[pl.BlockSpec(blk, lambda h, t: (t, SEG_AQ * N_HEADS + h)),
                  pl.BlockSpec(blk, lambda h, t: (t, SEG_AK * N_HEADS + h)),
                  pl.BlockSpec(blk, lambda h, t: (t, SEG_AV * N_HEADS + h))],
        out_specs=pl.BlockSpec(blk, lambda h, t: (t, h)),
        scratch_shapes=[pltpu.VMEM(blk, F32),
                        pltpu.VMEM((2,) + blk, F32),
                        pltpu.VMEM((2,) + blk, F32),
                        pltpu.VMEM((2,) + blk, BF16),
                        pltpu.VMEM((2,) + blk, BF16),
                        pltpu.VMEM((ATT_BLK, HEAD_DIM), BF16),
                        pltpu.VMEM((ATT_BLK, HEAD_DIM), BF16),
                        pltpu.VMEM((n_pat,) + blk, F32),
                        pltpu.VMEM((n_pat,) + blk, F32),
                        pltpu.VMEM((n_pat,) + blk, F32),
                        pltpu.VMEM(blk, F32),
                        pltpu.VMEM((2, ATT_BLK, 2 * ATT_BLK), F32)],
        compiler_params=_params("parallel", "arbitrary"),
        name="attn",
    )(proj, proj, proj)


REC_CHUNK = 128
REC_LEVELS = 7
REC_ROWS = 1024
REC_HEADS = 8


def _block_scans(lf):
    c = lf.shape[0]
    fwd, rev = [lf], [lf]
    x, y = lf, lf
    for j in range(REC_LEVELS):
        m = 1 << j
        if m < SUBLANES:
            x3 = x.reshape(c // SUBLANES, SUBLANES, HEAD_DIM)
            y3 = y.reshape(c // SUBLANES, SUBLANES, HEAD_DIM)
            sub = lax.broadcasted_iota(jnp.int32, x3.shape, 1)
            block0 = sub & ~(2 * m - 1)
            upper = (sub & m) != 0
            addx = jnp.where(upper, jnp.take_along_axis(x3, block0 + (m - 1), axis=1), 0.0)
            addy = jnp.where(upper, 0.0, jnp.take_along_axis(y3, block0 + m, axis=1))
            x = x + addx.reshape(c, HEAD_DIM)
            y = y + addy.reshape(c, HEAD_DIM)
        else:
            xs, ys = [], []
            for b in range(c // (2 * m)):
                lo = x[b * 2 * m: b * 2 * m + m]
                hi = x[b * 2 * m + m: (b + 1) * 2 * m]
                xs += [lo, hi + lo[m - 1:m, :]]
                lo = y[b * 2 * m: b * 2 * m + m]
                hi = y[b * 2 * m + m: (b + 1) * 2 * m]
                ys += [lo + hi[0:1, :], hi]
            x = jnp.concatenate(xs, axis=0)
            y = jnp.concatenate(ys, axis=0)
        fwd.append(x)
        rev.append(y)
    return fwd, rev


def _rec_kernel(q_ref, lf_ref, v_ref, g_ref, gw_ref, o_ref, state_ref):
    @pl.when(pl.program_id(1) == 0)
    def _():
        state_ref[...] = jnp.zeros_like(state_ref)

    c = REC_CHUNK
    ti = lax.broadcasted_iota(jnp.int32, (c, c), 0)
    si = lax.broadcasted_iota(jnp.int32, (c, c), 1)
    diff = ti ^ si
    level = jnp.full((c, c), -1, jnp.int32)
    for j in range(REC_LEVELS):
        level = jnp.where((diff >> j) == 1, j, level)
    level = jnp.where(si > ti, -1, jnp.where(si == ti, REC_LEVELS, level))
    nt_dims = (((1,), (1,)), ((), ()))

    def one_head(hh, rows):
        lanes = slice(hh * HEAD_DIM, (hh + 1) * HEAD_DIM)
        qf = q_ref[rows, lanes]
        lf = lf_ref[rows, lanes]
        v = v_ref[rows, lanes].astype(BF16)
        kk = 1.0 - jnp.exp2(lf)
        fwd, rev = _block_scans(lf)
        b = fwd[REC_LEVELS]

        a = jnp.where(level == REC_LEVELS,
                      lax.dot_general(qf.astype(BF16), kk.astype(BF16), nt_dims,
                                      preferred_element_type=F32), 0.0)
        for j in range(REC_LEVELS):
            m = 1 << j
            if m < SUBLANES:
                ql = (qf * jnp.exp2(fwd[j])).astype(BF16)
                kl = (kk * jnp.exp2(rev[j] - lf)).astype(BF16)
                al = lax.dot_general(ql, kl, nt_dims, preferred_element_type=F32)
                a = jnp.where(level == j, al, a)
                continue
            lows = [slice(b, b + m) for b in range(0, c, 2 * m)]
            ups = [slice(b + m, b + 2 * m) for b in range(0, c, 2 * m)]
            pick = lambda x, parts: jnp.concatenate([x[p] for p in parts], axis=0)
            ql = (pick(qf, ups) * jnp.exp2(pick(fwd[j], ups))).astype(BF16)
            k_low = pick(kk, lows) * jnp.exp2(pick(rev[j], lows) - pick(lf, lows))
            zeros = jnp.zeros((m, HEAD_DIM), F32)
            kl = jnp.concatenate(
                [piece for b in range(len(lows))
                 for piece in (k_low[b * m:(b + 1) * m], zeros)], axis=0).astype(BF16)
            al = lax.dot_general(ql, kl, nt_dims, preferred_element_type=F32)
            a_up = jnp.where(pick(level, ups) == j, al, pick(a, ups))
            a = jnp.concatenate(
                [piece for b in range(len(lows))
                 for piece in (a[lows[b]], a_up[b * m:(b + 1) * m])], axis=0)

        state_t = state_ref[hh]
        o = (lax.dot_general((qf * jnp.exp2(b)).astype(BF16), state_t.astype(BF16),
                             nt_dims, preferred_element_type=F32)
             + jnp.dot(a.astype(BF16), v, preferred_element_type=F32))
        b_last = b[c - 1:c, :]
        kd = (kk * jnp.exp2(b_last - b)).astype(BF16)
        upd_t = lax.dot_general(v, kd, (((0,), (0,)), ((), ())),
                                preferred_element_type=F32)
        state_ref[hh] = jnp.exp2(b_last) * state_t + upd_t

        o = o * lax.rsqrt(jnp.mean(o * o, axis=-1, keepdims=True) + EPS) * gw_ref[...]
        o_ref[rows, lanes] = (o * g_ref[rows, lanes]).astype(o_ref.dtype)

    def chunk(ci, carry):
        rows = pl.ds(pl.multiple_of(ci * c, c), c)
        for hh in range(REC_HEADS):
            one_head(hh, rows)
        return carry

    lax.fori_loop(0, REC_ROWS // c, chunk, 0, unroll=2)


def _recurrence(proj, gnorm_w):
    s = proj.shape[0]
    blk = (REC_ROWS, REC_HEADS * HEAD_DIM)
    groups = N_HEADS // REC_HEADS
    seg = lambda k: (lambda h, i: (i, k * groups + h))
    return pl.pallas_call(
        _rec_kernel,
        out_shape=jax.ShapeDtypeStruct((s, WIDTH), BF16),
        grid=(groups, s // REC_ROWS),
        in_specs=[pl.BlockSpec(blk, seg(SEG_HQ)),
                  pl.BlockSpec(blk, seg(SEG_HF)),
                  pl.BlockSpec(blk, seg(SEG_HI)),
                  pl.BlockSpec(blk, seg(SEG_HG)),
                  pl.BlockSpec((1, HEAD_DIM), lambda h, i: (0, 0))],
        out_specs=pl.BlockSpec(blk, lambda h, i: (i, h)),
        scratch_shapes=[pltpu.VMEM((REC_HEADS, HEAD_DIM, HEAD_DIM), F32)],
        compiler_params=_params("parallel", "arbitrary"),
        name="hgrn",
    )(proj, proj, proj, proj, gnorm_w)


OUT_ROWS = 256


def _out_kernel(x_ref, ya_ref, yr_ref, w_ref, g_ref, nw_ref, sh_ref, sc_ref,
                x1_ref, h2_ref):
    for r0 in range(0, x_ref.shape[0], OUT_ROWS):
        rows = slice(r0, r0 + OUT_ROWS)
        y = (jnp.dot(ya_ref[rows, :], w_ref[0:WIDTH, :], preferred_element_type=F32)
             + jnp.dot(yr_ref[rows, :], w_ref[WIDTH:, :], preferred_element_type=F32))
        x1 = x_ref[rows, :] + g_ref[...] * y
        x1_ref[rows, :] = x1
        h2_ref[rows, :] = _norm_modulate(x1, nw_ref[...], sc_ref[...], sh_ref[...]).astype(BF16)


def _out_proj(x2d, y_attn, y_rec, w_out_bf16, mod, norm2_w, tm=512):
    s, d = x2d.shape
    vec = lambda k: pl.BlockSpec((1, d), lambda i: (0, k))
    return pl.pallas_call(
        _out_kernel,
        out_shape=(jax.ShapeDtypeStruct((s, d), F32),
                   jax.ShapeDtypeStruct((s, d), BF16)),
        grid=(s // tm,),
        in_specs=[pl.BlockSpec((tm, d), lambda i: (i, 0)),
                  pl.BlockSpec((tm, WIDTH), lambda i: (i, 0)),
                  pl.BlockSpec((tm, WIDTH), lambda i: (i, 0)),
                  pl.BlockSpec((2 * WIDTH, d), lambda i: (0, 0)),
                  vec(2),
                  pl.BlockSpec((1, d), lambda i: (0, 0)),
                  vec(3), vec(4)],
        out_specs=(pl.BlockSpec((tm, d), lambda i: (i, 0)),
                   pl.BlockSpec((tm, d), lambda i: (i, 0))),
        compiler_params=_params("parallel"),
        name="out_proj",
    )(x2d, y_attn, y_rec, w_out_bf16, mod, norm2_w, mod, mod)


UP_ROWS = 256


def _conv_rows(stage, half, cols, cw_ref, cb_ref):
    w0, w1, w2 = cw_ref[0:1, cols], cw_ref[1:2, cols], cw_ref[2:3, cols]
    b = cb_ref[:, cols]
    r = [stage[pl.ds(SUBLANES - 2 + k, half, stride=2), :] for k in range(4)]
    even = r[2] * w2 + r[1] * w1 + r[0] * w0 + b
    odd = r[3] * w2 + r[2] * w1 + r[1] * w0 + b
    return even, odd


def _up_kernel(h_ref, wg_ref, wu_ref, cwg_ref, cwu_ref, cbg_ref, cbu_ref, o_ref,
               tailg_ref, tailu_ref, stage_ref, act_ref):
    @pl.when(pl.program_id(1) == 0)
    def _():
        tailg_ref[...] = jnp.zeros_like(tailg_ref)
        tailu_ref[...] = jnp.zeros_like(tailu_ref)

    tm, n = o_ref.shape
    half = tm // 2
    for c in range(n // SUB_N):
        cols = slice(c * SUB_N, (c + 1) * SUB_N)
        pieces = range(0, tm, UP_ROWS)
        g = jnp.concatenate([jnp.dot(h_ref[r0:r0 + UP_ROWS, :], wg_ref[:, cols],
                                     preferred_element_type=F32) for r0 in pieces], axis=0)
        u = jnp.concatenate([jnp.dot(h_ref[r0:r0 + UP_ROWS, :], wu_ref[:, cols],
                                     preferred_element_type=F32) for r0 in pieces], axis=0)
        for lh in range(SUB_N // LANES):
            lanes = slice(c * SUB_N + lh * LANES, c * SUB_N + (lh + 1) * LANES)
            sg = stage_ref.at[c % 2, 0, lh]
            su = stage_ref.at[c % 2, 1, lh]
            sg[0:SUBLANES, :] = tailg_ref[:, lanes]
            su[0:SUBLANES, :] = tailu_ref[:, lanes]
            sg[SUBLANES:, :] = g[:, lh * LANES:(lh + 1) * LANES]
            su[SUBLANES:, :] = u[:, lh * LANES:(lh + 1) * LANES]
            tailg_ref[:, lanes] = sg[tm:, :]
            tailu_ref[:, lanes] = su[tm:, :]
            plane = act_ref.at[lh]
            gates = _conv_rows(sg, half, lanes, cwg_ref, cbg_ref)
            ups = _conv_rows(su, half, lanes, cwu_ref, cbu_ref)
            for parity in range(2):
                plane[pl.ds(parity, half, stride=2), :] = _silu(gates[parity]) * ups[parity]
            o_ref[:, lanes] = plane[...].astype(o_ref.dtype)


def _ffn_up(h2, w_up_bf16, conv_w, conv_b, tm=1024, tn=D_FF // 2):
    s, d = h2.shape
    nj = D_FF // tn
    once = pl.Buffered(1)
    return pl.pallas_call(
        _up_kernel,
        out_shape=jax.ShapeDtypeStruct((s, D_FF), BF16),
        grid=(nj, s // tm),
        in_specs=[pl.BlockSpec((tm, d), lambda j, i: (i, 0)),
                  pl.BlockSpec((d, tn), lambda j, i: (0, j), pipeline_mode=once),
                  pl.BlockSpec((d, tn), lambda j, i: (0, nj + j), pipeline_mode=once),
                  pl.BlockSpec((CONV_WIDTH, tn), lambda j, i: (0, j)),
                  pl.BlockSpec((CONV_WIDTH, tn), lambda j, i: (0, nj + j)),
                  pl.BlockSpec((1, tn), lambda j, i: (0, j)),
                  pl.BlockSpec((1, tn), lambda j, i: (0, nj + j))],
        out_specs=pl.BlockSpec((tm, tn), lambda j, i: (i, j)),
        scratch_shapes=[pltpu.VMEM((SUBLANES, tn), F32),
                        pltpu.VMEM((SUBLANES, tn), F32),
                        pltpu.VMEM((2, 2, SUB_N // LANES, SUBLANES + tm, LANES), F32),
                        pltpu.VMEM((SUB_N // LANES, tm, LANES), F32)],
        compiler_params=_params("parallel", "arbitrary"),
        name="ffn_up",
    )(h2, w_up_bf16, w_up_bf16, conv_w, conv_w, conv_b, conv_b)


def _down_kernel(a_ref, w_ref, x1_ref, g_ref, fw_ref, o_ref):
    y = jnp.dot(a_ref[...], w_ref[...], preferred_element_type=F32)
    x2 = x1_ref[...] + g_ref[...] * y
    ms = jnp.mean(x2 * x2, axis=-1, keepdims=True)
    o_ref[...] = (x2 * lax.rsqrt(ms + EPS)) * fw_ref[...]


def _ffn_down(act, w_down_bf16, x1, mod, final_w, tm=512):
    s, d = x1.shape
    return pl.pallas_call(
        _down_kernel,
        out_shape=jax.ShapeDtypeStruct((s, d), F32),
        grid=(s // tm,),
        in_specs=[pl.BlockSpec((tm, D_FF), lambda i: (i, 0)),
                  pl.BlockSpec((D_FF, d), lambda i: (0, 0), pipeline_mode=pl.Buffered(1)),
                  pl.BlockSpec((tm, d), lambda i: (i, 0)),
                  pl.BlockSpec((1, d), lambda i: (0, 5)),
                  pl.BlockSpec((1, d), lambda i: (0, 0))],
        out_specs=pl.BlockSpec((tm, d), lambda i: (i, 0)),
        compiler_params=_params("parallel"),
        name="ffn_down",
    )(act, w_down_bf16, x1, mod, final_w)


def _rotary_inv_freq():
    inv = 1.0 / (ROPE_THETA ** (jnp.arange(0, ROT_DIM, 2, dtype=F32) / ROT_DIM))
    return jnp.tile(jnp.concatenate([inv, inv]), LANES // ROT_DIM).reshape(1, LANES)


def kernel(x, c, positions, w_ada, b_ada, norm1_w, w_in, lb_logits, gnorm_w, w_out,
           norm2_w, w_up, conv_w, conv_b, w_down, final_norm_w):
    b, s, d = x.shape
    assert b == 1 and d == D_MODEL and w_ada.shape[0] == 1
    x2d = x.reshape(s, d)
    pos_col = positions.reshape(s, 1)

    mod = _modulation(c.reshape(d, 1), w_ada[0], b_ada)
    proj = _in_proj(x2d, pos_col, _rotary_inv_freq(), norm1_w, mod, mod, lb_logits,
                    _arrange_w_in(w_in[0]))
    y_attn = _attention(proj)
    y_rec = _recurrence(proj, gnorm_w)
    x1, h2 = _out_proj(x2d, y_attn, y_rec, w_out[0].astype(BF16), mod, norm2_w)
    act = _ffn_up(h2, w_up[0].astype(BF16), conv_w[0], conv_b)
    out = _ffn_down(act, w_down[0].astype(BF16), x1, mod, final_norm_w.reshape(1, d))
    return out.reshape(b, s, d)
```
